```python
import jax, jax.numpy as jnp
from jax import lax
import numpy as np

D_MODEL = 1024
BATCH = 16
SEQ = 2048
DEPTH = 2
DEC_BATCH = 128
DEC_SEQ = 4
PAST_LEN = 16384
PAGE_SIZE = 128

SWA_HEADS = 8
SWA_KV_HEADS = 2
SWA_HEAD_DIM = 64
SWA_WINDOW = 128
ROPE_DIM = SWA_HEAD_DIM // 4
ROPE_THETA = 500000.0
LRU_WIDTH = 512
LRU_BLOCKS = 8
LRU_BLOCK_W = LRU_WIDTH // LRU_BLOCKS
LRU_CONV_W = 4
LRU_C = 8.0
GLA_HEADS = 4
GLA_DK = 64
GLA_DV = 128
GLA_RANK = 16
GLA_TAU = 16.0
GLA_CHUNK = 32
N_MEM = 256
MEM_HEADS = 4
MEM_HEAD_DIM = 128
D_FF = 2816
FFN_CONV_W = 3

EPS = 1e-6
NEG = -1e30
F32 = jnp.float32
SWA_Q = SWA_HEADS * SWA_HEAD_DIM
SWA_KV = SWA_KV_HEADS * SWA_HEAD_DIM
SWA_GROUP = SWA_HEADS // SWA_KV_HEADS
GLA_K = GLA_HEADS * GLA_DK
GLA_V = GLA_HEADS * GLA_DV
MEM_W = MEM_HEADS * MEM_HEAD_DIM
IN_SPLITS = (SWA_Q, SWA_KV, SWA_KV, LRU_WIDTH, LRU_WIDTH, GLA_K, GLA_K, GLA_V, GLA_V, GLA_RANK, D_MODEL, D_MODEL, D_MODEL)
IN_COLS = sum(IN_SPLITS)

kernel_name = 'hybrid_swa_rglru_gla_memxattn_convffn_step'


def rmsnorm(x, g):
    xf = x.astype(F32)
    y = xf * lax.rsqrt(jnp.mean(xf * xf, axis=-1, keepdims=True) + EPS)
    return (y * g.astype(F32)).astype(x.dtype)


def partial_rope(x, pos):
    half = ROPE_DIM // 2
    inv = ROPE_THETA ** (-jnp.arange(half, dtype=F32) * 2.0 / ROPE_DIM)
    ang = pos.astype(F32)[:, None] * inv[None, :]
    cos = jnp.cos(ang)[:, None, :]
    sin = jnp.sin(ang)[:, None, :]
    xr = x[..., :ROPE_DIM].astype(F32)
    x1, x2 = xr[..., :half], xr[..., half:]
    rot = jnp.concatenate([x1 * cos - x2 * sin, x2 * cos + x1 * sin], axis=-1).astype(x.dtype)
    return jnp.concatenate([rot, x[..., ROPE_DIM:]], axis=-1)


def sink_softmax(s, sink):
    m = jnp.maximum(jnp.max(s, axis=-1, keepdims=True), sink)
    e = jnp.exp(s - m)
    return e / (jnp.sum(e, axis=-1, keepdims=True) + jnp.exp(sink - m))


def swa_banded(q, k, v, sink):
    B, T = q.shape[:2]
    W, KV, G, hd = SWA_WINDOW, SWA_KV_HEADS, SWA_GROUP, SWA_HEAD_DIM
    nb = T // W
    qb = q.reshape(B, nb, W, KV, G, hd)

    def band(z):
        prev = jnp.concatenate([jnp.zeros_like(z[:, :W]), z[:, :-W]], axis=1)
        return jnp.concatenate([prev.reshape(B, nb, W, KV, hd), z.reshape(B, nb, W, KV, hd)], axis=2)

    kb, vb = band(k), band(v)
    s = jnp.einsum('bnqkgd,bnskd->bnkgqs', qb.astype(F32), kb.astype(F32)) * SWA_HEAD_DIM ** -0.5
    qi = jnp.arange(W)[:, None] + W
    ki = jnp.arange(2 * W)[None, :]
    rel = qi - ki
    band_ok = (rel >= 0) & (rel <= W)
    blk = jnp.arange(nb)[:, None, None]
    mask = band_ok[None] & ((blk > 0) | (ki >= W)[None])
    s = jnp.where(mask[None, :, None, None], s, NEG)
    pr = sink_softmax(s, sink.astype(F32).reshape(KV, G)[None, None, :, :, None, None])
    o = jnp.einsum('bnkgqs,bnskd->bnqkgd', pr.astype(v.dtype), vb)
    return o.reshape(B, T, SWA_Q)


def swa_step(q, k, v, k_buf, v_buf, sink):
    B, S = q.shape[:2]
    W, KV, G = k_buf.shape[1], SWA_KV_HEADS, SWA_GROUP
    kk = jnp.concatenate([k_buf.astype(k.dtype), k], axis=1)
    vv = jnp.concatenate([v_buf.astype(v.dtype), v], axis=1)
    qg = q.reshape(B, S, KV, G, SWA_HEAD_DIM)
    s = jnp.einsum('bqkgd,bskd->bkgqs', qg.astype(F32), kk.astype(F32)) * SWA_HEAD_DIM ** -0.5
    rel = (jnp.arange(S)[:, None] + W) - jnp.arange(W + S)[None, :]
    mask = (rel >= 0) & (rel <= SWA_WINDOW)
    s = jnp.where(mask, s, NEG)
    pr = sink_softmax(s, sink.astype(F32).reshape(KV, G)[None, :, :, None, None])
    o = jnp.einsum('bkgqs,bskd->bqkgd', pr.astype(vv.dtype), vv).reshape(B, S, SWA_Q)
    return o, kk[:, -SWA_WINDOW:], vv[:, -SWA_WINDOW:]


def causal_dwconv(x, buf, w, b):
    K = w.shape[0]
    T = x.shape[1]
    xp = jnp.concatenate([buf.astype(x.dtype), x], axis=1)
    y = b + sum(xp[:, j:j + T] * w[j] for j in range(K))
    return y, xp[:, T:]


def _lin_combine(left, right):
    a_l, b_l = left
    a_r, b_r = right
    return a_l * a_r, a_r * b_l + b_r


def rg_lru(x, h0, wa, ba, wx, bx, lam):
    B, T, _ = x.shape
    xf = x.astype(F32)
    xb = xf.reshape(B, T, LRU_BLOCKS, LRU_BLOCK_W)
    r = jax.nn.sigmoid(jnp.einsum('btni,nij->btnj', xb, wa.astype(F32)).reshape(B, T, LRU_WIDTH) + ba.astype(F32))
    i = jax.nn.sigmoid(jnp.einsum('btni,nij->btnj', xb, wx.astype(F32)).reshape(B, T, LRU_WIDTH) + bx.astype(F32))
    log_a = -LRU_C * r * jax.nn.softplus(-lam.astype(F32))
    a = jnp.exp(log_a)
    b = jnp.sqrt(-jnp.expm1(2.0 * log_a)) * (i * xf)
    b = b.at[:, 0].add(a[:, 0] * h0.astype(F32))
    _, h = lax.associative_scan(_lin_combine, (a, b), axis=1)
    return h.astype(x.dtype), h[:, -1].astype(x.dtype)


def gla(q, k, v, log_alpha, s0):
    B, T, H, DK = q.shape
    DV = v.shape[-1]
    C = GLA_CHUNK if T % GLA_CHUNK == 0 else T
    n = T // C

    def to_chunks(z):
        return z.astype(F32).reshape(B, n, C, H, z.shape[-1]).transpose(1, 0, 3, 2, 4)

    qc, kc, vc, gc = to_chunks(q), to_chunks(k), to_chunks(v), to_chunks(log_alpha)
    causal = jnp.tril(jnp.ones((C, C), dtype=bool))

    def step(S, inp):
        qi, ki, vi, gi = inp
        cum = jnp.cumsum(gi, axis=2)
        q_t = qi * jnp.exp(cum)
        k_t = ki * jnp.exp(-cum)
        att = jnp.where(causal, jnp.einsum('bhtd,bhsd->bhts', q_t, k_t), 0.0)
        o = jnp.einsum('bhts,bhsv->bhtv', att, vi) + jnp.einsum('bhtd,bhdv->bhtv', q_t, S)
        last = cum[:, :, -1:, :]
        S_new = jnp.exp(last[:, :, 0, :])[..., None] * S + jnp.einsum('bhsd,bhsv->bhdv', ki * jnp.exp(last - cum), vi)
        return S_new, o

    S, o = lax.scan(step, s0.astype(F32), (qc, kc, vc, gc))
    o = o.transpose(1, 0, 3, 2, 4).reshape(B, T, H, DV)
    return o, S.astype(s0.dtype)


def mixer_block(x, pos, p, prompt, k_buf, v_buf, lru_h, lru_buf, gla_s):
    B, T, _ = x.shape
    h = rmsnorm(x, p['norm_mix_g'])
    offs = np.cumsum(IN_SPLITS)[:-1].tolist()
    (q_a, k_a, v_a, x_b, y_b, q_c, k_c, v_c, r_c, a_c, g_a, g_b, g_c) = jnp.split(h @ p['w_in'], offs, axis=-1)

    q = partial_rope(rmsnorm(q_a.reshape(B, T, SWA_HEADS, SWA_HEAD_DIM), p['swa_qn_g']), pos)
    k = partial_rope(rmsnorm(k_a.reshape(B, T, SWA_KV_HEADS, SWA_HEAD_DIM), p['swa_kn_g']), pos)
    v = v_a.reshape(B, T, SWA_KV_HEADS, SWA_HEAD_DIM)
    if prompt:
        o_a = swa_banded(q, k, v, p['swa_sink'])
        new_kb, new_vb = k[:, -SWA_WINDOW:], v[:, -SWA_WINDOW:]
    else:
        o_a, new_kb, new_vb = swa_step(q, k, v, k_buf, v_buf, p['swa_sink'])

    xc, new_lru_buf = causal_dwconv(x_b, lru_buf, p['lru_conv_w'], p['lru_conv_b'])
    hr, new_h = rg_lru(xc, lru_h, p['lru_wa'], p['lru_ba'], p['lru_wx'], p['lru_bx'], p['lru_lambda'])
    o_b = jax.nn.gelu(y_b) * hr

    log_alpha = jax.nn.log_sigmoid((a_c @ p['gla_wa2'] + p['gla_ba']).astype(F32)) / GLA_TAU
    o, new_s = gla(q_c.reshape(B, T, GLA_HEADS, GLA_DK) * GLA_DK ** -0.5,
                   k_c.reshape(B, T, GLA_HEADS, GLA_DK),
                   v_c.reshape(B, T, GLA_HEADS, GLA_DV),
                   log_alpha.reshape(B, T, GLA_HEADS, GLA_DK), gla_s)
    o_c = rmsnorm(o, p['gla_on_g']).astype(x.dtype).reshape(B, T, GLA_V) * jax.nn.silu(r_c)

    merged = (jax.nn.sigmoid(g_a) * (o_a @ p['w_branch_a'])
              + jax.nn.sigmoid(g_b) * (o_b @ p['w_branch_b'])
              + jax.nn.sigmoid(g_c) * (o_c @ p['w_branch_c']))
    return merged @ p['w_out'], (new_kb, new_vb, new_h, new_lru_buf, new_s)


def memory_kv(mem, p):
    B = mem.shape[0]
    m = rmsnorm(mem, p['norm_mem_g'])
    k = rmsnorm((m @ p['x_wk']).reshape(B, -1, MEM_HEADS, MEM_HEAD_DIM), p['x_kn_g'])
    v = (m @ p['x_wv']).reshape(B, -1, MEM_HEADS, MEM_HEAD_DIM)
    return k, v


def cross_attn(h, mk, mv, p):
    B, T, _ = h.shape
    q = rmsnorm((h @ p['x_wq']).reshape(B, T, MEM_HEADS, MEM_HEAD_DIM), p['x_qn_g'])
    s = jnp.einsum('bthd,bshd->bhts', q.astype(F32), mk.astype(F32)) * MEM_HEAD_DIM ** -0.5
    pr = jax.nn.softmax(s, axis=-1)
    o = jnp.einsum('bhts,bshd->bthd', pr.astype(mv.dtype), mv).reshape(B, T, MEM_W)
    return o @ p['x_wo']


def conv_ffn(h, buf, p):
    u = h @ p['ffn_w_up']
    u, new_buf = causal_dwconv(u, buf, p['ffn_conv_w'], p['ffn_conv_b'])
    g, val = jnp.split(u, 2, axis=-1)
    return (jax.nn.silu(g) * val) @ p['ffn_w_down'], new_buf


def decoder_layer(x, pos, mk, mv, p, prompt, k_buf, v_buf, lru_h, lru_buf, gla_s, ffn_buf):
    mix, (nkb, nvb, nh, nlb, ns) = mixer_block(x, pos, p, prompt, k_buf, v_buf, lru_h, lru_buf, gla_s)
    x = x + mix
    x = x + cross_attn(rmsnorm(x, p['norm_x_g']), mk, mv, p)
    f, nfb = conv_ffn(rmsnorm(x, p['norm_ffn_g']), ffn_buf, p)
    x = x + f
    return x, (nkb, nvb, nh, nlb, ns, nfb)


def setup_inputs(seed: int = 0) -> dict:
    key = jax.random.key(seed)
    keys = iter(jax.random.split(key, 64))

    def nrm(shape, scale=1.0):
        return jax.random.normal(next(keys), shape, F32) * scale

    def gain(shape):
        return 1.0 + nrm(shape, 0.02)

    L, D = DEPTH, D_MODEL
    u = jax.random.uniform(next(keys), (L, LRU_WIDTH), F32, 0.9, 0.999)
    a = u ** (1.0 / LRU_C)
    return {
        'x_prompt': nrm((BATCH, SEQ, D)),
        'x_sample': nrm((DEC_BATCH, DEC_SEQ, D)),
        'cache_swa_k': nrm((L, DEC_BATCH, SWA_WINDOW, SWA_KV_HEADS, SWA_HEAD_DIM)),
        'cache_swa_v': nrm((L, DEC_BATCH, SWA_WINDOW, SWA_KV_HEADS, SWA_HEAD_DIM)),
        'state_lru_h': nrm((L, DEC_BATCH, LRU_WIDTH), 0.5),
        'state_lru_conv': nrm((L, DEC_BATCH, LRU_CONV_W - 1, LRU_WIDTH)),
        'state_gla_s': nrm((L, DEC_BATCH, GLA_HEADS, GLA_DK, GLA_DV), 0.5),
        'cache_mem_k': nrm((L, DEC_BATCH, N_MEM, MEM_HEADS, MEM_HEAD_DIM)),
        'cache_mem_v': nrm((L, DEC_BATCH, N_MEM, MEM_HEADS, MEM_HEAD_DIM)),
        'state_ffn_conv': nrm((L, DEC_BATCH, FFN_CONV_W - 1, 2 * D_FF)),
        'mem_prompt': nrm((BATCH, N_MEM, D)),
        'norm_mix_g': gain((L, D)),
        'w_in': nrm((L, D, IN_COLS), D ** -0.5),
        'swa_qn_g': gain((L, SWA_HEAD_DIM)),
        'swa_kn_g': gain((L, SWA_HEAD_DIM)),
        'swa_sink': nrm((L, SWA_HEADS), 0.5),
        'lru_conv_w': nrm((L, LRU_CONV_W, LRU_WIDTH), LRU_CONV_W ** -0.5),
        'lru_conv_b': nrm((L, LRU_WIDTH), 0.01),
        'lru_wa': nrm((L, LRU_BLOCKS, LRU_BLOCK_W, LRU_BLOCK_W), LRU_BLOCK_W ** -0.5),
        'lru_ba': nrm((L, LRU_WIDTH), 0.01),
        'lru_wx': nrm((L, LRU_BLOCKS, LRU_BLOCK_W, LRU_BLOCK_W), LRU_BLOCK_W ** -0.5),
        'lru_bx': nrm((L, LRU_WIDTH), 0.01),
        'lru_lambda': jnp.log(a) - jnp.log1p(-a),
        'gla_wa2': nrm((L, GLA_RANK, GLA_K), GLA_RANK ** -0.5),
        'gla_ba': nrm((L, GLA_K), 0.01),
        'gla_on_g': gain((L, GLA_DV)),
        'w_branch_a': nrm((L, SWA_Q, D), SWA_Q ** -0.5),
        'w_branch_b': nrm((L, LRU_WIDTH, D), LRU_WIDTH ** -0.5),
        'w_branch_c': nrm((L, GLA_V, D), GLA_V ** -0.5),
        'w_out': nrm((L, D, D), D ** -0.5),
        'norm_x_g': gain((L, D)),
        'norm_mem_g': gain((L, D)),
        'x_wq': nrm((L, D, MEM_W), D ** -0.5),
        'x_wk': nrm((L, D, MEM_W), D ** -0.5),
        'x_wv': nrm((L, D, MEM_W), D ** -0.5),
        'x_qn_g': gain((L, MEM_HEAD_DIM)),
        'x_kn_g': gain((L, MEM_HEAD_DIM)),
        'x_wo': nrm((L, MEM_W, D), MEM_W ** -0.5),
        'norm_ffn_g': gain((L, D)),
        'ffn_w_up': nrm((L, D, 2 * D_FF), D ** -0.5),
        'ffn_conv_w': nrm((L, FFN_CONV_W, 2 * D_FF), FFN_CONV_W ** -0.5),
        'ffn_conv_b': nrm((L, 2 * D_FF), 0.01),
        'ffn_w_down': nrm((L, D_FF, D), D_FF ** -0.5),
    }


def reference(x_prompt, x_sample, cache_swa_k, cache_swa_v, state_lru_h, state_lru_conv, state_gla_s,
              cache_mem_k, cache_mem_v, state_ffn_conv, mem_prompt,
              norm_mix_g, w_in, swa_qn_g, swa_kn_g, swa_sink, lru_conv_w, lru_conv_b, lru_wa, lru_ba,
              lru_wx, lru_bx, lru_lambda, gla_wa2, gla_ba, gla_on_g, w_branch_a, w_branch_b, w_branch_c,
              w_out, norm_x_g, norm_mem_g, x_wq, x_wk, x_wv, x_qn_g, x_kn_g, x_wo, norm_ffn_g,
              ffn_w_up, ffn_conv_w, ffn_conv_b, ffn_w_down):
    P = dict(norm_mix_g=norm_mix_g, w_in=w_in, swa_qn_g=swa_qn_g, swa_kn_g=swa_kn_g, swa_sink=swa_sink,
             lru_conv_w=lru_conv_w, lru_conv_b=lru_conv_b, lru_wa=lru_wa, lru_ba=lru_ba, lru_wx=lru_wx,
             lru_bx=lru_bx, lru_lambda=lru_lambda, gla_wa2=gla_wa2, gla_ba=gla_ba, gla_on_g=gla_on_g,
             w_branch_a=w_branch_a, w_branch_b=w_branch_b, w_branch_c=w_branch_c, w_out=w_out,
             norm_x_g=norm_x_g, norm_mem_g=norm_mem_g, x_wq=x_wq, x_wk=x_wk, x_wv=x_wv, x_qn_g=x_qn_g,
             x_kn_g=x_kn_g, x_wo=x_wo, norm_ffn_g=norm_ffn_g, ffn_w_up=ffn_w_up, ffn_conv_w=ffn_conv_w,
             ffn_conv_b=ffn_conv_b, ffn_w_down=ffn_w_down)
    Bp, Tp = x_prompt.shape[:2]
    Ts = x_sample.shape[1]
    dt = x_prompt.dtype
    pos_p = jnp.arange(Tp, dtype=jnp.int32)
    pos_s = PAST_LEN + jnp.arange(Ts, dtype=jnp.int32)
    xp, xs = x_prompt, x_sample
    sp, ss, mks, mvs = [], [], [], []
    for l in range(DEPTH):
        p = {name: arr[l] for name, arr in P.items()}
        mk, mv = memory_kv(mem_prompt, p)
        xp, st = decoder_layer(xp, pos_p, mk, mv, p, True, None, None,
                               jnp.zeros((Bp, LRU_WIDTH), dt),
                               jnp.zeros((Bp, LRU_CONV_W - 1, LRU_WIDTH), dt),
                               jnp.zeros((Bp, GLA_HEADS, GLA_DK, GLA_DV), dt),
                               jnp.zeros((Bp, FFN_CONV_W - 1, 2 * D_FF), dt))
        sp.append(st)
        mks.append(mk)
        mvs.append(mv)
        xs, st = decoder_layer(xs, pos_s, cache_mem_k[l], cache_mem_v[l], p, False,
                               cache_swa_k[l], cache_swa_v[l], state_lru_h[l], state_lru_conv[l],
                               state_gla_s[l], state_ffn_conv[l])
        ss.append(st)
    return (xp, xs,
            jnp.stack([s[0] for s in sp]), jnp.stack([s[1] for s in sp]), jnp.stack([s[2] for s in sp]),
            jnp.stack([s[3] for s in sp]), jnp.stack([s[4] for s in sp]), jnp.stack(mks), jnp.stack(mvs),
            jnp.stack([s[5] for s in sp]),
            jnp.stack([s[0] for s in ss]), jnp.stack([s[1] for s in ss]), jnp.stack([s[2] for s in ss]),
            jnp.stack([s[3] for s in ss]), jnp.stack([s[4] for s in ss]), jnp.stack([s[5] for s in ss]))
```

```python
import functools
import math

import numpy as np
import jax
import jax.numpy as jnp
from jax import lax
from jax.experimental import pallas as pl
from jax.experimental.pallas import tpu as pltpu

F32 = jnp.float32
BF16 = jnp.bfloat16

D_MODEL = 1024
PAST_LEN = 16384
SWA_HEADS = 8
SWA_KV_HEADS = 2
SWA_HEAD_DIM = 64
SWA_WINDOW = 128
ROPE_DIM = 16
ROPE_THETA = 500000.0
LRU_WIDTH = 512
LRU_BLOCKS = 8
LRU_CONV_W = 4
LRU_C = 8.0
GLA_HEADS = 4
GLA_DK = 64
GLA_DV = 128
GLA_RANK = 16
GLA_TAU = 16.0
GLA_CHUNK = 32
MEM_HEADS = 4
MEM_HEAD_DIM = 128
D_FF = 2816
FFN_CONV_W = 3
EPS = 1e-6
NEG = -1e30

SWA_Q = SWA_HEADS * SWA_HEAD_DIM
SWA_KV = SWA_KV_HEADS * SWA_HEAD_DIM
GLA_K = GLA_HEADS * GLA_DK
GLA_V = GLA_HEADS * GLA_DV
MEM_W = MEM_HEADS * MEM_HEAD_DIM

LANES = 128
VMEM_LIMIT = 56 * 1024 * 1024

C_Q = 0
C_KV = 512
C_XB = 768
C_YB = 1280
C_GQK = 1792
C_GV = 2304
C_GR = 2816
C_AC = 3328
C_GATE = 3456
IN_COLS_PAD = 6528
AC_PAD = LANES - GLA_RANK


def _cparams(sem, vmem=VMEM_LIMIT):
    return pltpu.CompilerParams(dimension_semantics=sem, vmem_limit_bytes=vmem)


def _full(shape):
    n = len(shape)
    return pl.BlockSpec(shape, lambda *_: (0,) * n)


def _rows(tm, c):
    return pl.BlockSpec((tm, c), lambda i: (i, 0))


def _rms(x, g):
    ms = jnp.mean(x * x, axis=-1, keepdims=True)
    return x * lax.rsqrt(ms + EPS) * g


def _sigmoid(x):
    return 1.0 / (1.0 + jnp.exp(-x))


def _silu(x):
    return x * _sigmoid(x)


def _gelu_tanh(x):
    return 0.5 * x * (1.0 + jnp.tanh(math.sqrt(2.0 / math.pi) * (x + 0.044715 * (x * x * x))))


def _softplus(z):
    return jnp.maximum(z, 0.0) + jnp.log1p(jnp.exp(-jnp.abs(z)))


def _neg_expm1(x):
    p = 1.0 + x * (1.0 / 8.0)
    for n in (7.0, 6.0, 5.0, 4.0, 3.0, 2.0):
        p = 1.0 + x * (1.0 / n) * p
    return jnp.where(jnp.abs(x) < 0.25, -(x * p), 1.0 - jnp.exp(x))


def _block_ones(n, blk_r, blk_c=None):
    blk_c = blk_r if blk_c is None else blk_c
    m = np.kron(np.eye(n, dtype=np.float32), np.ones((blk_r, blk_c), np.float32))
    return jnp.asarray(m, dtype=BF16)


def _shift_rows(x, d, row):
    del row
    return pltpu.roll(x, d, 0)


def _rope(x, rc, rs1, rs2):
    return x * rc + pltpu.roll(x, LANES - 8, 1) * rs1 + pltpu.roll(x, 8, 1) * rs2


def _in_proj_kernel(x_ref, g_ref, w_ref, e_ref, qg_ref, kg_ref, rc_ref, rs1_ref, rs2_ref,
                    wa2_ref, ba_ref,
                    q_ref, k_ref, v_ref, xb_ref, gy_ref, gq_ref, gk_ref, gv_ref, sr_ref, la_ref,
                    gate_ref):
    h = _rms(x_ref[...], g_ref[...]).astype(BF16)

    def mm(c0, c1):
        return jnp.dot(h, w_ref[:, c0:c1], preferred_element_type=F32)

    rc, rs1, rs2 = rc_ref[...], rs1_ref[...], rs2_ref[...]
    inv_hd = 1.0 / SWA_HEAD_DIM

    qf = mm(C_Q, C_Q + SWA_Q)
    ss = jnp.dot((qf * qf).astype(BF16), e_ref[...], preferred_element_type=F32)
    qn = qf * lax.rsqrt(ss * inv_hd + EPS)
    for p in range(SWA_Q // LANES):
        qp = qn[:, p * LANES:(p + 1) * LANES] * qg_ref[...]
        q_ref[p] = (_rope(qp, rc, rs1, rs2) * (SWA_HEAD_DIM ** -0.5)).astype(BF16)

    kv = mm(C_KV, C_KV + 2 * SWA_KV)
    kf = kv[:, :SWA_KV]
    ssk = jnp.dot((kf * kf).astype(BF16), e_ref[0:SWA_KV, 0:SWA_KV], preferred_element_type=F32)
    kn = kf * lax.rsqrt(ssk * inv_hd + EPS) * kg_ref[...]
    k_ref[...] = _rope(kn, rc, rs1, rs2)
    v_ref[...] = kv[:, SWA_KV:]

    xb_ref[...] = mm(C_XB, C_XB + LRU_WIDTH).astype(BF16)
    gy_ref[...] = _gelu_tanh(mm(C_YB, C_YB + LRU_WIDTH)).astype(BF16)

    qk = mm(C_GQK, C_GQK + 2 * GLA_K)
    gq_ref[...] = qk[:, :GLA_K] * (GLA_DK ** -0.5)
    gk_ref[...] = qk[:, GLA_K:]
    gv_ref[...] = mm(C_GV, C_GV + GLA_V).astype(BF16)
    sr_ref[...] = _silu(mm(C_GR, C_GR + GLA_V)).astype(BF16)

    ac = mm(C_AC, C_AC + LANES).astype(BF16)
    z = jnp.dot(ac, wa2_ref[...], preferred_element_type=F32) + ba_ref[...]
    la_ref[...] = -_softplus(-z) * (1.0 / GLA_TAU)

    for c in range(3 * D_MODEL // 512):
        gate_ref[:, c * 512:(c + 1) * 512] = _sigmoid(
            mm(C_GATE + c * 512, C_GATE + (c + 1) * 512)).astype(BF16)


def _in_proj(x, w, rope_tabs, tm):
    m = x.shape[0]
    rc, rs1, rs2 = rope_tabs
    ntab = rc.shape[0] // tm
    tab_spec = pl.BlockSpec((tm, LANES), lambda i: (i % ntab, 0))
    outs = [
        jax.ShapeDtypeStruct((SWA_Q // LANES, m, LANES), BF16),
        jax.ShapeDtypeStruct((m, SWA_KV), F32),
        jax.ShapeDtypeStruct((m, SWA_KV), F32),
        jax.ShapeDtypeStruct((m, LRU_WIDTH), BF16),
        jax.ShapeDtypeStruct((m, LRU_WIDTH), BF16),
        jax.ShapeDtypeStruct((m, GLA_K), F32),
        jax.ShapeDtypeStruct((m, GLA_K), F32),
        jax.ShapeDtypeStruct((m, GLA_V), BF16),
        jax.ShapeDtypeStruct((m, GLA_V), BF16),
        jax.ShapeDtypeStruct((m, GLA_K), F32),
        jax.ShapeDtypeStruct((m, 3 * D_MODEL), BF16),
    ]
    out_specs = [
        pl.BlockSpec((SWA_Q // LANES, tm, LANES), lambda i: (0, i, 0)),
        _rows(tm, SWA_KV), _rows(tm, SWA_KV), _rows(tm, LRU_WIDTH), _rows(tm, LRU_WIDTH),
        _rows(tm, GLA_K), _rows(tm, GLA_K), _rows(tm, GLA_V), _rows(tm, GLA_V), _rows(tm, GLA_K),
        _rows(tm, 3 * D_MODEL),
    ]
    in_specs = [
        _rows(tm, D_MODEL), _full((1, D_MODEL)), _full((D_MODEL, IN_COLS_PAD)),
        _full((SWA_Q, SWA_Q)), _full((1, LANES)), _full((1, LANES)),
        tab_spec, tab_spec, tab_spec,
        _full((LANES, GLA_K)), _full((1, GLA_K)),
    ]
    return pl.pallas_call(
        _in_proj_kernel, grid=(m // tm,), in_specs=in_specs, out_specs=out_specs, out_shape=outs,
        compiler_params=_cparams(("parallel",)), name="in_proj",
    )(x, w["norm_mix_g"], w["w_in"], w["e_swa"], w["swa_qn_g"], w["swa_kn_g"], rc, rs1, rs2,
      w["gla_wa2"], w["gla_ba"])


def _sink_softmax(s, sk):
    m = jnp.maximum(jnp.max(s, axis=-1, keepdims=True), sk)
    e = jnp.exp(s - m)
    den = jnp.sum(e, axis=-1, keepdims=True) + jnp.exp(sk - m)
    return e / den


def _swa_prompt_kernel(q_ref, kc_ref, kp_ref, vc_ref, vp_ref, sink_ref, o_ref, *, nblk):
    j = pl.program_id(1)
    w = SWA_WINDOW
    lane = lax.broadcasted_iota(jnp.int32, (1, LANES), 1)
    lo = lane < SWA_HEAD_DIM

    def dup(x, kvh):
        r = pltpu.roll(x, SWA_HEAD_DIM, 1)
        y = jnp.where(lo, x, r) if kvh == 0 else jnp.where(lo, r, x)
        return y.astype(BF16)

    kall = jnp.concatenate([kp_ref[...], kc_ref[...]], axis=0)
    vall = jnp.concatenate([vp_ref[...], vc_ref[...]], axis=0)
    kd = [dup(kall, 0), dup(kall, 1)]
    vd = [dup(vall, 0), dup(vall, 1)]

    qi = lax.broadcasted_iota(jnp.int32, (w, 2 * w), 0) + w
    ki = lax.broadcasted_iota(jnp.int32, (w, 2 * w), 1)
    rel = qi - ki
    band = (rel >= 0) & (rel <= w)
    first = band & ((ki >= w) | ((jnp.zeros_like(ki) + j) > 0))
    row = lax.broadcasted_iota(jnp.int32, (2 * w, 1), 0)

    for c in range(nblk):
        ok = first if c == 0 else band
        ok2 = jnp.concatenate([ok, ok], axis=0)
        for kvh in range(SWA_KV_HEADS):
            kk = kd[kvh][c * w:(c + 2) * w]
            vv = vd[kvh][c * w:(c + 2) * w]
            qg = q_ref[2 * kvh:2 * kvh + 2, c * w:(c + 1) * w, :].reshape(2 * w, LANES)
            res = []
            for par in range(2):
                qm = jnp.where(lo if par == 0 else jnp.logical_not(lo), qg, jnp.zeros_like(qg))
                s = lax.dot_general(qm, kk, (((1,), (1,)), ((), ())), preferred_element_type=F32)
                s = jnp.where(ok2, s, NEG)
                h0 = kvh * 4 + par
                sk = jnp.where(row < w, sink_ref[h0], sink_ref[h0 + 2])
                p = _sink_softmax(s, sk).astype(BF16)
                res.append(jnp.dot(p, vv, preferred_element_type=F32))
            oo = jnp.where(lo, res[0], res[1]).astype(BF16)
            o_ref[2 * kvh:2 * kvh + 2, c * w:(c + 1) * w, :] = oo.reshape(2, w, LANES)


def _swa_prompt(q4, k, v, sink, b, t):
    m = b * t
    w = SWA_WINDOW
    tq = min(512, t)
    nj = t // tq
    nblk = tq // w
    r = t // w
    cur = lambda bb, j: (bb * nj + j, 0)
    prev = lambda bb, j: (jnp.maximum(bb * r + j * nblk - 1, 0), 0)
    qspec = pl.BlockSpec((4, tq, LANES), lambda bb, j: (0, bb * nj + j, 0))
    return pl.pallas_call(
        functools.partial(_swa_prompt_kernel, nblk=nblk),
        grid=(b, nj),
        in_specs=[qspec, pl.BlockSpec((tq, LANES), cur), pl.BlockSpec((w, LANES), prev),
                  pl.BlockSpec((tq, LANES), cur), pl.BlockSpec((w, LANES), prev),
                  pl.BlockSpec(memory_space=pltpu.SMEM)],
        out_specs=qspec,
        out_shape=jax.ShapeDtypeStruct((4, m, LANES), BF16),
        compiler_params=_cparams(("parallel", "parallel")), name="swa_prompt",
    )(q4, k, k, v, v, sink)


def _swa_step_kernel(q_ref, kc_ref, kn_ref, vc_ref, vn_ref, sk_ref, o_ref, *, bb, ts):
    w = SWA_WINDOW
    nk = w + 8
    row_t = lax.broadcasted_iota(jnp.int32, (SWA_HEADS * ts, nk), 0) % ts
    key = lax.broadcasted_iota(jnp.int32, (SWA_HEADS * ts, nk), 1)
    ok = ((key < w) & (key >= row_t)) | ((key >= w) & ((key - w) <= row_t))
    sk = sk_ref[...]
    for i in range(bb):
        kk = jnp.concatenate([kc_ref[i], kn_ref[i]], axis=0).astype(BF16)
        vv = jnp.concatenate([vc_ref[i], vn_ref[i]], axis=0).astype(BF16)
        s = lax.dot_general(q_ref[i], kk, (((1,), (1,)), ((), ())), preferred_element_type=F32)
        s = jnp.where(ok, s, NEG)
        p = _sink_softmax(s, sk).astype(BF16)
        o_ref[i] = jnp.dot(p, vv, preferred_element_type=F32)


def _swa_step(qm, kc, kn8, vc, vn8, sink_rows, ts):
    b = qm.shape[0]
    bb = 8
    nq = SWA_HEADS * ts
    w = SWA_WINDOW
    blk = lambda r: pl.BlockSpec((bb, r, LANES), lambda i: (i, 0, 0))
    return pl.pallas_call(
        functools.partial(_swa_step_kernel, bb=bb, ts=ts),
        grid=(b // bb,),
        in_specs=[blk(nq), blk(w), blk(8), blk(w), blk(8), _full((nq, 1))],
        out_specs=blk(nq),
        out_shape=jax.ShapeDtypeStruct((b, nq, LANES), F32),
        compiler_params=_cparams(("parallel",)), name="swa_step",
    )(qm, kc, kn8, vc, vn8, sink_rows)


def _lru_kernel(xb_ref, gy_ref, hist_ref, h0_ref, cw_ref, cb_ref, wa_ref, ba_ref, wx_ref, bx_ref,
                lam_ref, ob_ref, hl_ref, co_ref, cx_sc, ch_sc, *, s, tm):
    j = pl.program_id(1)
    kw = LRU_CONV_W

    @pl.when(j == 0)
    def _():
        cx_sc[...] = hist_ref[...]
        ch_sc[...] = h0_ref[...]

    x = xb_ref[...].astype(F32)
    row = lax.broadcasted_iota(jnp.int32, (tm, 1), 0)

    xc = cb_ref[...] + x * cw_ref[kw - 1:kw, :]
    for k in range(1, kw):
        d = k * s
        sh = pltpu.roll(x, d, 0)
        if s == 1:
            for r in range(k):
                sh = jnp.where(row == r, cx_sc[kw - 1 - k + r:kw - k + r, :], sh)
        else:
            fill = jnp.concatenate([cx_sc[(kw - 1 - k) * s:(kw - 1) * s, :],
                                    jnp.zeros((tm - d, LRU_WIDTH), F32)], axis=0)
            sh = jnp.where(row < d, fill, sh)
        xc = xc + sh * cw_ref[kw - 1 - k:kw - k, :]
    tail = x[tm - (kw - 1) * s:, :] if s > 1 else x[tm - 8:, :][8 - (kw - 1):, :]
    cx_sc[...] = tail
    co_ref[...] = tail

    xcb = xc.astype(BF16)
    r_g = _sigmoid(jnp.dot(xcb, wa_ref[...], preferred_element_type=F32) + ba_ref[...])
    i_g = _sigmoid(jnp.dot(xcb, wx_ref[...], preferred_element_type=F32) + bx_ref[...])
    log_a = (-LRU_C) * r_g * _softplus(-lam_ref[...])
    a = jnp.exp(log_a)
    bv = jnp.sqrt(_neg_expm1(2.0 * log_a)) * (i_g * xc)

    if s == 1:
        bv = bv + jnp.where(row < 1, a * ch_sc[...], 0.0)
    else:
        hp = jnp.concatenate([ch_sc[...], jnp.zeros((tm - s, LRU_WIDTH), F32)], axis=0)
        bv = bv + a * hp

    d = s
    while d < tm:
        keep = row >= d
        a_sh = jnp.where(keep, pltpu.roll(a, d, 0), 1.0)
        b_sh = jnp.where(keep, pltpu.roll(bv, d, 0), 0.0)
        bv = a * b_sh + bv
        a = a * a_sh
        d *= 2

    hlast = bv[tm - s:, :] if s > 1 else bv[tm - 8:, :][7:, :]
    ch_sc[...] = hlast
    hl_ref[...] = hlast
    ob_ref[...] = (gy_ref[...].astype(F32) * bv).astype(BF16)


def _lru(xb, gy, hist, h0, w, s, tm, ngroups):
    m = xb.shape[0]
    nj = m // (ngroups * tm)
    kw = LRU_CONV_W
    rowspec = pl.BlockSpec((tm, LRU_WIDTH), lambda g, j: (g * nj + j, 0))
    gspec = lambda r: pl.BlockSpec((None, r, LRU_WIDTH), lambda g, j: (g, 0, 0))
    full2 = lambda shape: pl.BlockSpec(shape, lambda g, j: (0, 0))
    return pl.pallas_call(
        functools.partial(_lru_kernel, s=s, tm=tm),
        grid=(ngroups, nj),
        in_specs=[rowspec, rowspec, gspec((kw - 1) * s), gspec(s),
                  full2((kw, LRU_WIDTH)), full2((1, LRU_WIDTH)),
                  full2((LRU_WIDTH, LRU_WIDTH)), full2((1, LRU_WIDTH)),
                  full2((LRU_WIDTH, LRU_WIDTH)), full2((1, LRU_WIDTH)), full2((1, LRU_WIDTH))],
        out_specs=[rowspec, gspec(s), gspec((kw - 1) * s)],
        out_shape=[jax.ShapeDtypeStruct((m, LRU_WIDTH), BF16),
                   jax.ShapeDtypeStruct((ngroups, s, LRU_WIDTH), F32),
                   jax.ShapeDtypeStruct((ngroups, (kw - 1) * s, LRU_WIDTH), F32)],
        scratch_shapes=[pltpu.VMEM(((kw - 1) * s, LRU_WIDTH), F32), pltpu.VMEM((s, LRU_WIDTH), F32)],
        compiler_params=_cparams(("arbitrary", "arbitrary")), name="lru",
    )(xb, gy, hist, h0, w["lru_conv_w"], w["lru_conv_b"], w["lru_wa"], w["lru_ba"], w["lru_wx"],
      w["lru_bx"], w["lru_lambda"])


def _gla_prompt_kernel(q_ref, k_ref, v_ref, sr_ref, la_ref, g_ref, o_ref, st_ref, s_sc, *, tm):
    j = pl.program_id(1)
    c32 = GLA_CHUNK
    nc = tm // c32

    @pl.when(j == 0)
    def _():
        s_sc[...] = jnp.zeros_like(s_sc)

    row = lax.broadcasted_iota(jnp.int32, (tm, 1), 0)
    rin = row % c32
    cum = la_ref[...]
    d = 1
    while d < c32:
        cum = cum + jnp.where(rin >= d, pltpu.roll(cum, d, 0), 0.0)
        d *= 2
    cum3 = cum.reshape(nc, c32, GLA_K)
    last3 = cum3[:, c32 - 1:c32, :]
    k_all = k_ref[...]
    qt = q_ref[...] * jnp.exp(cum)
    kt = (k_all * jnp.exp(-cum)).astype(BF16)
    kl = (k_all.reshape(nc, c32, GLA_K) * jnp.exp(last3 - cum3)).astype(BF16)
    dec = jnp.exp(last3)

    ti = lax.broadcasted_iota(jnp.int32, (tm, tm), 0)
    si = lax.broadcasted_iota(jnp.int32, (tm, tm), 1)
    causal = (si <= ti) & ((ti // c32) == (si // c32))
    lane = lax.broadcasted_iota(jnp.int32, (1, LANES), 1)
    lo = lane < GLA_DK

    for p in range(GLA_HEADS // 2):
        sl = slice(p * LANES, (p + 1) * LANES)
        qtp = qt[:, sl]
        ktp = kt[:, sl]
        for par in range(2):
            h = 2 * p + par
            hs = slice(h * GLA_DV, (h + 1) * GLA_DV)
            qm = jnp.where(lo if par == 0 else jnp.logical_not(lo), qtp, 0.0).astype(BF16)
            att = lax.dot_general(qm, ktp, (((1,), (1,)), ((), ())), preferred_element_type=F32)
            att = jnp.where(causal, att, 0.0).astype(BF16)
            vh = v_ref[:, hs]
            o_intra = jnp.dot(att, vh, preferred_element_type=F32)
            st = s_sc[h]
            outs = []
            for c in range(nc):
                rs = slice(c * c32, (c + 1) * c32)
                oi = lax.dot_general(qm[rs], st.astype(BF16), (((1,), (1,)), ((), ())),
                                     preferred_element_type=F32)
                outs.append(o_intra[rs] + oi)
                ut = lax.dot_general(vh[rs], kl[c][:, sl], (((0,), (0,)), ((), ())),
                                     preferred_element_type=F32)
                st = st * dec[c][:, sl] + ut
            s_sc[h] = st
            o = jnp.concatenate(outs, axis=0)
            on = _rms(o, g_ref[...])
            o_ref[:, hs] = (on * sr_ref[:, hs].astype(F32)).astype(BF16)
    st_ref[...] = s_sc[...]


def _gla_prompt(gq, gk, gv, sr, la, on_g, b, t):
    m = b * t
    tm = min(256, t)
    nj = t // tm
    rs = lambda c: pl.BlockSpec((tm, c), lambda bb, j: (bb * nj + j, 0))
    sspec = pl.BlockSpec((None, GLA_HEADS, GLA_DV, LANES), lambda bb, j: (bb, 0, 0, 0))
    return pl.pallas_call(
        functools.partial(_gla_prompt_kernel, tm=tm),
        grid=(b, nj),
        in_specs=[rs(GLA_K), rs(GLA_K), rs(GLA_V), rs(GLA_V), rs(GLA_K),
                  pl.BlockSpec((1, GLA_DV), lambda bb, j: (0, 0))],
        out_specs=[rs(GLA_V), sspec],
        out_shape=[jax.ShapeDtypeStruct((m, GLA_V), BF16),
                   jax.ShapeDtypeStruct((b, GLA_HEADS, GLA_DV, LANES), F32)],
        scratch_shapes=[pltpu.VMEM((GLA_HEADS, GLA_DV, LANES), F32)],
        compiler_params=_cparams(("arbitrary", "arbitrary")), name="gla_prompt",
    )(gq, gk, gv, sr, la, on_g)


def _gla_step_kernel(q_ref, k_ref, v_ref, sr_ref, la_ref, s0_ref, g_ref, o_ref, sn_ref, *, bb, ts):
    row = lax.broadcasted_iota(jnp.int32, (8, 1), 0)
    ti = lax.broadcasted_iota(jnp.int32, (8, 8), 0)
    si = lax.broadcasted_iota(jnp.int32, (8, 8), 1)
    causal = si <= ti
    lane = lax.broadcasted_iota(jnp.int32, (1, LANES), 1)
    lo = lane < GLA_DK
    for i in range(bb):
        cum = la_ref[i]
        d = 1
        while d < ts:
            cum = cum + jnp.where(row >= d, pltpu.roll(cum, d, 0), 0.0)
            d *= 2
        last = cum[ts - 1:ts, :]
        k_all = k_ref[i]
        qt = q_ref[i] * jnp.exp(cum)
        kt = (k_all * jnp.exp(-cum)).astype(BF16)
        kl = (k_all * jnp.exp(last - cum)).astype(BF16)
        dec = jnp.exp(last)
        for p in range(GLA_HEADS // 2):
            sl = slice(p * LANES, (p + 1) * LANES)
            sp = s0_ref[i, p]
            dcol = jnp.broadcast_to(dec[:, sl], (LANES, LANES)).T
            spb = sp.astype(BF16)
            klp = kl[:, sl]
            new_rows = []
            for par in range(2):
                h = 2 * p + par
                hs = slice(h * GLA_DV, (h + 1) * GLA_DV)
                qm = jnp.where(lo if par == 0 else jnp.logical_not(lo), qt[:, sl], 0.0).astype(BF16)
                att = lax.dot_general(qm, kt[:, sl], (((1,), (1,)), ((), ())),
                                      preferred_element_type=F32)
                att = jnp.where(causal, att, 0.0).astype(BF16)
                vh = v_ref[i][:, hs]
                o = jnp.dot(att, vh, preferred_element_type=F32) + jnp.dot(
                    qm, spb, preferred_element_type=F32)
                on = _rms(o, g_ref[...])
                o_ref[i, :, hs] = (on * sr_ref[i][:, hs].astype(F32)).astype(BF16)
                u = lax.dot_general(klp, vh, (((0,), (0,)), ((), ())), preferred_element_type=F32)
                new_rows.append(u[par * GLA_DK:(par + 1) * GLA_DK, :])
            sn_ref[i, p] = sp * dcol + jnp.concatenate(new_rows, axis=0)


def _gla_step(gq8, gk8, gv8, sr8, la8, s0p, on_g, ts):
    b = gq8.shape[0]
    bb = 8
    blk = lambda c: pl.BlockSpec((bb, 8, c), lambda i: (i, 0, 0))
    sspec = pl.BlockSpec((bb, GLA_HEADS // 2, LANES, GLA_DV), lambda i: (i, 0, 0, 0))
    return pl.pallas_call(
        functools.partial(_gla_step_kernel, bb=bb, ts=ts),
        grid=(b // bb,),
        in_specs=[blk(GLA_K), blk(GLA_K), blk(GLA_V), blk(GLA_V), blk(GLA_K), sspec,
                  _full((1, GLA_DV))],
        out_specs=[blk(GLA_V), sspec],
        out_shape=[jax.ShapeDtypeStruct((b, 8, GLA_V), BF16),
                   jax.ShapeDtypeStruct((b, GLA_HEADS // 2, LANES, GLA_DV), F32)],
        compiler_params=_cparams(("parallel",)), name="gla_step",
    )(gq8, gk8, gv8, sr8, la8, s0p, on_g)


def _merge_kernel(oa_ref, ob_ref, oc_ref, gate_ref, x_ref, wa_ref, wb_ref, wc_ref, wo_ref,
                  nx_ref, wq_ref, qn_ref, x1_ref, qx_ref):
    oa = jnp.concatenate([oa_ref[p] for p in range(SWA_Q // LANES)], axis=-1)
    dm = D_MODEL
    merged = (gate_ref[:, 0:dm].astype(F32) * jnp.dot(oa, wa_ref[...], preferred_element_type=F32)
              + gate_ref[:, dm:2 * dm].astype(F32)
              * jnp.dot(ob_ref[...], wb_ref[...], preferred_element_type=F32)
              + gate_ref[:, 2 * dm:3 * dm].astype(F32)
              * jnp.dot(oc_ref[...], wc_ref[...], preferred_element_type=F32))
    x1 = x_ref[...] + jnp.dot(merged.astype(BF16), wo_ref[...], preferred_element_type=F32)
    x1_ref[...] = x1
    hx = _rms(x1, nx_ref[...]).astype(BF16)
    q = jnp.dot(hx, wq_ref[...], preferred_element_type=F32)
    for h in range(MEM_HEADS):
        hs = slice(h * MEM_HEAD_DIM, (h + 1) * MEM_HEAD_DIM)
        qx_ref[:, hs] = _rms(q[:, hs], qn_ref[...]).astype(BF16)


def _merge(oa4, ob, oc, gates, x, w, tm):
    m = x.shape[0]
    return pl.pallas_call(
        _merge_kernel, grid=(m // tm,),
        in_specs=[pl.BlockSpec((SWA_Q // LANES, tm, LANES), lambda i: (0, i, 0)),
                  _rows(tm, LRU_WIDTH), _rows(tm, GLA_V), _rows(tm, 3 * D_MODEL), _rows(tm, D_MODEL),
                  _full((SWA_Q, D_MODEL)), _full((LRU_WIDTH, D_MODEL)), _full((GLA_V, D_MODEL)),
                  _full((D_MODEL, D_MODEL)), _full((1, D_MODEL)), _full((D_MODEL, MEM_W)),
                  _full((1, MEM_HEAD_DIM))],
        out_specs=[_rows(tm, D_MODEL), _rows(tm, MEM_W)],
        out_shape=[jax.ShapeDtypeStruct((m, D_MODEL), F32), jax.ShapeDtypeStruct((m, MEM_W), BF16)],
        compiler_params=_cparams(("parallel",)), name="merge",
    )(oa4, ob, oc, gates, x, w["w_branch_a"], w["w_branch_b"], w["w_branch_c"], w["w_out"],
      w["norm_x_g"], w["x_wq"], w["x_qn_g"])


def _memkv_kernel(m_ref, g_ref, wk_ref, wv_ref, kn_ref, k_ref, v_ref):
    hm = _rms(m_ref[...], g_ref[...]).astype(BF16)
    k = jnp.dot(hm, wk_ref[...], preferred_element_type=F32)
    for h in range(MEM_HEADS):
        hs = slice(h * MEM_HEAD_DIM, (h + 1) * MEM_HEAD_DIM)
        k_ref[:, hs] = _rms(k[:, hs], kn_ref[...])
    v_ref[...] = jnp.dot(hm, wv_ref[...], preferred_element_type=F32)


def _memkv(mem, w, tm):
    m = mem.shape[0]
    return pl.pallas_call(
        _memkv_kernel, grid=(m // tm,),
        in_specs=[_rows(tm, D_MODEL), _full((1, D_MODEL)), _full((D_MODEL, MEM_W)),
                  _full((D_MODEL, MEM_W)), _full((1, MEM_HEAD_DIM))],
        out_specs=[_rows(tm, MEM_W), _rows(tm, MEM_W)],
        out_shape=[jax.ShapeDtypeStruct((m, MEM_W), F32), jax.ShapeDtypeStruct((m, MEM_W), F32)],
        compiler_params=_cparams(("parallel",)), name="memkv",
    )(mem, w["norm_mem_g"], w["x_wk"], w["x_wv"], w["x_kn_g"])


def _xattn_heads(q, mk, mv):
    outs = []
    for h in range(MEM_HEADS):
        hs = slice(h * MEM_HEAD_DIM, (h + 1) * MEM_HEAD_DIM)
        s = lax.dot_general(q[:, hs], mk[:, hs].astype(BF16), (((1,), (1,)), ((), ())),
                            preferred_element_type=F32) * (MEM_HEAD_DIM ** -0.5)
        m = jnp.max(s, axis=-1, keepdims=True)
        e = jnp.exp(s - m)
        p = (e / jnp.sum(e, axis=-1, keepdims=True)).astype(BF16)
        outs.append(jnp.dot(p, mv[:, hs].astype(BF16), preferred_element_type=F32))
    return jnp.concatenate(outs, axis=-1)


def _xattn_prompt_kernel(q_ref, mk_ref, mv_ref, o_ref):
    o_ref[...] = _xattn_heads(q_ref[...], mk_ref[...], mv_ref[...]).astype(BF16)


def _xattn_prompt(qx, mk, mv, b, t):
    m = b * t
    n_mem = mk.shape[0] // b
    tq = min(512, t)
    nj = t // tq
    qspec = pl.BlockSpec((tq, MEM_W), lambda bb, j: (bb * nj + j, 0))
    mspec = pl.BlockSpec((n_mem, MEM_W), lambda bb, j: (bb, 0))
    return pl.pallas_call(
        _xattn_prompt_kernel, grid=(b, nj), in_specs=[qspec, mspec, mspec], out_specs=qspec,
        out_shape=jax.ShapeDtypeStruct((m, MEM_W), BF16),
        compiler_params=_cparams(("parallel", "parallel")), name="xattn_prompt",
    )(qx, mk, mv)


def _xattn_step_kernel(q_ref, mk_ref, mv_ref, o_ref, *, bb):
    for i in range(bb):
        o_ref[i] = _xattn_heads(q_ref[i], mk_ref[i], mv_ref[i]).astype(BF16)


def _xattn_step(qx8, mk, mv):
    b, n_mem = mk.shape[0], mk.shape[1]
    bb = 8
    qspec = pl.BlockSpec((bb, 8, MEM_W), lambda i: (i, 0, 0))
    mspec = pl.BlockSpec((bb, n_mem, MEM_W), lambda i: (i, 0, 0))
    return pl.pallas_call(
        functools.partial(_xattn_step_kernel, bb=bb), grid=(b // bb,),
        in_specs=[qspec, mspec, mspec], out_specs=qspec,
        out_shape=jax.ShapeDtypeStruct((b, 8, MEM_W), BF16),
        compiler_params=_cparams(("parallel",)), name="xattn_step",
    )(qx8, mk, mv)


def _xo_up_kernel(ox_ref, x_ref, wo_ref, g_ref, wu_ref, x2_ref, u_ref):
    x2 = x_ref[...] + jnp.dot(ox_ref[...], wo_ref[...], preferred_element_type=F32)
    x2_ref[...] = x2
    hf = _rms(x2, g_ref[...]).astype(BF16)
    for c in range(2 * D_FF // 512):
        cs = slice(c * 512, (c + 1) * 512)
        u_ref[:, cs] = jnp.dot(hf, wu_ref[:, cs], preferred_element_type=F32).astype(BF16)


def _xo_up(ox, x1, w, tm):
    m = x1.shape[0]
    return pl.pallas_call(
        _xo_up_kernel, grid=(m // tm,),
        in_specs=[_rows(tm, MEM_W), _rows(tm, D_MODEL), _full((MEM_W, D_MODEL)), _full((1, D_MODEL)),
                  _full((D_MODEL, 2 * D_FF))],
        out_specs=[_rows(tm, D_MODEL), _rows(tm, 2 * D_FF)],
        out_shape=[jax.ShapeDtypeStruct((m, D_MODEL), F32),
                   jax.ShapeDtypeStruct((m, 2 * D_FF), BF16)],
        compiler_params=_cparams(("parallel",)), name="xo_up",
    )(ox, x1, w["x_wo"], w["norm_ffn_g"], w["ffn_w_up"])


def _ffn_down_kernel(u_ref, hist_ref, x_ref, cw_ref, cb_ref, wd_ref, x3_ref, co_ref, hs_sc, act_sc,
                     *, s, tm):
    j = pl.program_id(1)
    kw = FFN_CONV_W
    cwid = 256

    @pl.when(j == 0)
    def _():
        hs_sc[...] = hist_ref[...]

    row = lax.broadcasted_iota(jnp.int32, (tm, 1), 0)

    def conv(c0):
        cs = slice(c0, c0 + cwid)
        x = u_ref[:, cs].astype(F32)
        y = cb_ref[:, cs] + x * cw_ref[kw - 1:kw, cs]
        for k in range(1, kw):
            d = k * s
            sh = pltpu.roll(x, d, 0)
            if s == 1:
                for r in range(k):
                    sh = jnp.where(row == r, hs_sc[kw - 1 - k + r:kw - k + r, cs], sh)
            else:
                fill = jnp.concatenate([hs_sc[(kw - 1 - k) * s:(kw - 1) * s, cs],
                                        jnp.zeros((tm - d, cwid), F32)], axis=0)
                sh = jnp.where(row < d, fill, sh)
            y = y + sh * cw_ref[kw - 1 - k:kw - k, cs]
        tail = x[tm - (kw - 1) * s:, :] if s > 1 else x[tm - 8:, :][8 - (kw - 1):, :]
        return y, tail

    tails = {}
    for c in range(D_FF // cwid):
        g, tg = conv(c * cwid)
        val, tv = conv(D_FF + c * cwid)
        tails[c * cwid] = tg
        tails[D_FF + c * cwid] = tv
        act_sc[:, c * cwid:(c + 1) * cwid] = (_silu(g) * val).astype(BF16)
    for c0, tl in tails.items():
        hs_sc[:, c0:c0 + cwid] = tl
        co_ref[:, c0:c0 + cwid] = tl
    x3_ref[...] = x_ref[...] + jnp.dot(act_sc[...], wd_ref[...], preferred_element_type=F32)


def _ffn_down(u, hist, x2, w, s, tm, ngroups):
    m = x2.shape[0]
    nj = m // (ngroups * tm)
    kw = FFN_CONV_W
    rs = lambda c: pl.BlockSpec((tm, c), lambda g, j: (g * nj + j, 0))
    gspec = pl.BlockSpec((None, (kw - 1) * s, 2 * D_FF), lambda g, j: (g, 0, 0))
    full2 = lambda shape: pl.BlockSpec(shape, lambda g, j: (0, 0))
    return pl.pallas_call(
        functools.partial(_ffn_down_kernel, s=s, tm=tm),
        grid=(ngroups, nj),
        in_specs=[rs(2 * D_FF), gspec, rs(D_MODEL), full2((kw, 2 * D_FF)), full2((1, 2 * D_FF)),
                  full2((D_FF, D_MODEL))],
        out_specs=[rs(D_MODEL), gspec],
        out_shape=[jax.ShapeDtypeStruct((m, D_MODEL), F32),
                   jax.ShapeDtypeStruct((ngroups, (kw - 1) * s, 2 * D_FF), F32)],
        scratch_shapes=[pltpu.VMEM(((kw - 1) * s, 2 * D_FF), F32), pltpu.VMEM((tm, D_FF), BF16)],
        compiler_params=_cparams(("arbitrary", "arbitrary")), name="ffn_down",
    )(u, hist, x2, w["ffn_conv_w"], w["ffn_conv_b"], w["ffn_w_down"])


def _rope_tables(pos):
    half = ROPE_DIM // 2
    n = pos.shape[0]
    inv = ROPE_THETA ** (-jnp.arange(half, dtype=F32) * 2.0 / ROPE_DIM)
    ang = pos.astype(F32)[:, None] * inv[None, :]
    cos, sin = jnp.cos(ang), jnp.sin(ang)
    rest = SWA_HEAD_DIM - ROPE_DIM
    z8 = jnp.zeros((n, half), F32)
    zr = jnp.zeros((n, rest), F32)
    rc = jnp.concatenate([cos, cos, jnp.ones((n, rest), F32)], axis=1)
    rs1 = jnp.concatenate([-sin, z8, zr], axis=1)
    rs2 = jnp.concatenate([z8, sin, zr], axis=1)
    rep = LANES // SWA_HEAD_DIM
    return tuple(jnp.tile(a, (1, rep)) for a in (rc, rs1, rs2))


def _block_diag(w):
    n, bi, bj = w.shape
    eye = jnp.eye(n, dtype=w.dtype)
    return (w[:, :, None, :] * eye[:, None, :, None]).reshape(n * bi, n * bj)


def _layer_weights(p, l):
    w_in = p["w_in"][l]
    split = 3344
    w_in_p = jnp.concatenate(
        [w_in[:, :split], jnp.zeros((D_MODEL, AC_PAD), F32), w_in[:, split:]], axis=1).astype(BF16)
    wa2 = jnp.concatenate([p["gla_wa2"][l], jnp.zeros((AC_PAD, GLA_K), F32)], axis=0).astype(BF16)
    row = lambda a: a[l][None, :]
    return dict(
        norm_mix_g=row(p["norm_mix_g"]), w_in=w_in_p, e_swa=_block_ones(SWA_HEADS, SWA_HEAD_DIM),
        swa_qn_g=jnp.tile(p["swa_qn_g"][l], LANES // SWA_HEAD_DIM)[None, :],
        swa_kn_g=jnp.tile(p["swa_kn_g"][l], LANES // SWA_HEAD_DIM)[None, :],
        gla_wa2=wa2, gla_ba=row(p["gla_ba"]),
        lru_conv_w=p["lru_conv_w"][l], lru_conv_b=row(p["lru_conv_b"]),
        lru_wa=_block_diag(p["lru_wa"][l]).astype(BF16), lru_ba=row(p["lru_ba"]),
        lru_wx=_block_diag(p["lru_wx"][l]).astype(BF16), lru_bx=row(p["lru_bx"]),
        lru_lambda=row(p["lru_lambda"]), gla_on_g=row(p["gla_on_g"]),
        w_branch_a=p["w_branch_a"][l].astype(BF16), w_branch_b=p["w_branch_b"][l].astype(BF16),
        w_branch_c=p["w_branch_c"][l].astype(BF16), w_out=p["w_out"][l].astype(BF16),
        norm_x_g=row(p["norm_x_g"]), norm_mem_g=row(p["norm_mem_g"]),
        x_wq=p["x_wq"][l].astype(BF16), x_wk=p["x_wk"][l].astype(BF16),
        x_wv=p["x_wv"][l].astype(BF16), x_qn_g=row(p["x_qn_g"]), x_kn_g=row(p["x_kn_g"]),
        x_wo=p["x_wo"][l].astype(BF16), norm_ffn_g=row(p["norm_ffn_g"]),
        ffn_w_up=p["ffn_w_up"][l].astype(BF16), ffn_conv_w=p["ffn_conv_w"][l],
        ffn_conv_b=row(p["ffn_conv_b"]), ffn_w_down=p["ffn_w_down"][l].astype(BF16),
        swa_sink=p["swa_sink"][l],
    )


def _prompt_layer(x, mem, w, tabs, b, t):
    tm = min(512, t)
    n_mem = mem.shape[0] // b
    (q4, k, v, xb, gy, gq, gk, gv, sr, la, gates) = _in_proj(x, w, tabs, tm)
    oa4 = _swa_prompt(q4, k, v, w["swa_sink"], b, t)
    ob, h_last, conv_l = _lru(xb, gy, jnp.zeros((b, LRU_CONV_W - 1, LRU_WIDTH), F32),
                              jnp.zeros((b, 1, LRU_WIDTH), F32), w, 1, min(256, t), b)
    oc, st = _gla_prompt(gq, gk, gv, sr, la, w["gla_on_g"], b, t)
    x1, qx = _merge(oa4, ob, oc, gates, x, w, tm)
    mk, mv = _memkv(mem, w, min(512, mem.shape[0]))
    ox = _xattn_prompt(qx, mk, mv, b, t)
    x2, u = _xo_up(ox, x1, w, tm)
    x3, fconv = _ffn_down(u, jnp.zeros((b, FFN_CONV_W - 1, 2 * D_FF), F32), x2, w, 1, tm, b)

    k3 = k.reshape(b, t, SWA_KV_HEADS, SWA_HEAD_DIM)[:, -SWA_WINDOW:]
    v3 = v.reshape(b, t, SWA_KV_HEADS, SWA_HEAD_DIM)[:, -SWA_WINDOW:]
    st5 = st.reshape(b, GLA_HEADS // 2, 2, GLA_DV, 2, GLA_DK)
    s_new = jnp.stack([st5[:, :, 0, :, 0, :], st5[:, :, 1, :, 1, :]], axis=2)
    s_new = s_new.reshape(b, GLA_HEADS, GLA_DV, GLA_DK).transpose(0, 1, 3, 2)
    states = (k3, v3, h_last.reshape(b, LRU_WIDTH), conv_l, s_new,
              mk.reshape(b, n_mem, MEM_HEADS, MEM_HEAD_DIM),
              mv.reshape(b, n_mem, MEM_HEADS, MEM_HEAD_DIM), fconv)
    return x3, states


def _tm_to_bm(a, bs, ts):
    return a.reshape(ts, bs, a.shape[-1]).transpose(1, 0, 2)


def _pad8(a):
    return jnp.concatenate([a, jnp.zeros((a.shape[0], 8 - a.shape[1], a.shape[2]), a.dtype)], axis=1)


def _sample_layer(x, w, tabs, st_in, bs, ts):
    (ck, cv, lru_h, lru_conv, gla_s, mem_k, mem_v, ffn_conv) = st_in
    m = bs * ts
    tm = m
    (q4, k, v, xb, gy, gq, gk, gv, sr, la, gates) = _in_proj(x, w, tabs, tm)

    q8 = q4.reshape(4, ts, bs, 2, SWA_HEAD_DIM).transpose(2, 0, 3, 1, 4)
    q8 = q8.reshape(bs, SWA_HEADS, ts, SWA_HEAD_DIM)
    zq = jnp.zeros_like(q8[:, :4])
    qm = jnp.concatenate([jnp.concatenate([q8[:, :4], zq], axis=-1),
                          jnp.concatenate([zq, q8[:, 4:]], axis=-1)], axis=1)
    qm = qm.reshape(bs, SWA_HEADS * ts, LANES)
    k_new = _tm_to_bm(k, bs, ts)
    v_new = _tm_to_bm(v, bs, ts)
    ckf = ck.reshape(bs, SWA_WINDOW, SWA_KV)
    cvf = cv.reshape(bs, SWA_WINDOW, SWA_KV)
    sink_rows = jnp.repeat(w["swa_sink"], ts)[:, None]
    o32 = _swa_step(qm, ckf, _pad8(k_new), cvf, _pad8(v_new), sink_rows, ts)
    o8 = o32.reshape(bs, SWA_HEADS, ts, 2, SWA_HEAD_DIM)
    osel = jnp.concatenate([o8[:, :4, :, 0], o8[:, 4:, :, 1]], axis=1)
    oa4 = osel.reshape(bs, 4, 2, ts, SWA_HEAD_DIM).transpose(1, 3, 0, 2, 4)
    oa4 = oa4.reshape(4, m, LANES).astype(BF16)
    new_k = jnp.concatenate([ckf, k_new], axis=1)[:, -SWA_WINDOW:]
    new_v = jnp.concatenate([cvf, v_new], axis=1)[:, -SWA_WINDOW:]

    hist = lru_conv.transpose(1, 0, 2).reshape(1, (LRU_CONV_W - 1) * bs, LRU_WIDTH)
    ob, h_last, conv_l = _lru(xb, gy, hist, lru_h[None], w, bs, m, 1)
    new_conv = conv_l.reshape(LRU_CONV_W - 1, bs, LRU_WIDTH).transpose(1, 0, 2)

    bm8 = lambda a: _pad8(_tm_to_bm(a, bs, ts))
    s0p = gla_s.reshape(bs, GLA_HEADS // 2, 2 * GLA_DK, GLA_DV)
    oc8, sn = _gla_step(bm8(gq), bm8(gk), bm8(gv), bm8(sr), bm8(la), s0p, w["gla_on_g"], ts)
    oc = oc8[:, :ts].transpose(1, 0, 2).reshape(m, GLA_V)
    new_s = sn.reshape(bs, GLA_HEADS, GLA_DK, GLA_DV)

    x1, qx = _merge(oa4, ob, oc, gates, x, w, tm)
    n_mem = mem_k.shape[1]
    ox8 = _xattn_step(bm8(qx), mem_k.reshape(bs, n_mem, MEM_W), mem_v.reshape(bs, n_mem, MEM_W))
    ox = ox8[:, :ts].transpose(1, 0, 2).reshape(m, MEM_W)
    x2, u = _xo_up(ox, x1, w, tm)
    fh = ffn_conv.transpose(1, 0, 2).reshape(1, (FFN_CONV_W - 1) * bs, 2 * D_FF)
    x3, fconv = _ffn_down(u, fh, x2, w, bs, m, 1)
    new_fconv = fconv.reshape(FFN_CONV_W - 1, bs, 2 * D_FF).transpose(1, 0, 2)

    states = (new_k.reshape(bs, SWA_WINDOW, SWA_KV_HEADS, SWA_HEAD_DIM),
              new_v.reshape(bs, SWA_WINDOW, SWA_KV_HEADS, SWA_HEAD_DIM),
              h_last.reshape(bs, LRU_WIDTH), new_conv, new_s, new_fconv)
    return x3, states


def kernel(x_prompt, x_sample, cache_swa_k, cache_swa_v, state_lru_h, state_lru_conv, state_gla_s, cache_mem_k, cache_mem_v, state_ffn_conv, mem_prompt, norm_mix_g, w_in, swa_qn_g, swa_kn_g, swa_sink, lru_conv_w, lru_conv_b, lru_wa, lru_ba, lru_wx, lru_bx, lru_lambda, gla_wa2, gla_ba, gla_on_g, w_branch_a, w_branch_b, w_branch_c, w_out, norm_x_g, norm_mem_g, x_wq, x_wk, x_wv, x_qn_g, x_kn_g, x_wo, norm_ffn_g, ffn_w_up, ffn_conv_w, ffn_conv_b, ffn_w_down):
    p = dict(norm_mix_g=norm_mix_g, w_in=w_in, swa_qn_g=swa_qn_g, swa_kn_g=swa_kn_g, swa_sink=swa_sink,
             lru_conv_w=lru_conv_w, lru_conv_b=lru_conv_b, lru_wa=lru_wa, lru_ba=lru_ba, lru_wx=lru_wx,
             lru_bx=lru_bx, lru_lambda=lru_lambda, gla_wa2=gla_wa2, gla_ba=gla_ba, gla_on_g=gla_on_g,
             w_branch_a=w_branch_a, w_branch_b=w_branch_b, w_branch_c=w_branch_c, w_out=w_out,
             norm_x_g=norm_x_g, norm_mem_g=norm_mem_g, x_wq=x_wq, x_wk=x_wk, x_wv=x_wv, x_qn_g=x_qn_g,
             x_kn_g=x_kn_g, x_wo=x_wo, norm_ffn_g=norm_ffn_g, ffn_w_up=ffn_w_up, ffn_conv_w=ffn_conv_w,
             ffn_conv_b=ffn_conv_b, ffn_w_down=ffn_w_down)
    depth = w_in.shape[0]
    bp, tp = x_prompt.shape[:2]
    bs, ts = x_sample.shape[:2]
    tabs_p = _rope_tables(jnp.arange(tp, dtype=jnp.int32))
    tabs_s = _rope_tables(jnp.repeat(PAST_LEN + jnp.arange(ts, dtype=jnp.int32), bs))

    xp = x_prompt.reshape(bp * tp, D_MODEL)
    xs = x_sample.transpose(1, 0, 2).reshape(ts * bs, D_MODEL)
    mem = mem_prompt.reshape(-1, D_MODEL)
    sp, ss = [], []
    for l in range(depth):
        w = _layer_weights(p, l)
        xp, st = _prompt_layer(xp, mem, w, tabs_p, bp, tp)
        sp.append(st)
        xs, st = _sample_layer(xs, w, tabs_s,
                               (cache_swa_k[l], cache_swa_v[l], state_lru_h[l], state_lru_conv[l],
                                state_gla_s[l], cache_mem_k[l], cache_mem_v[l], state_ffn_conv[l]),
                               bs, ts)
        ss.append(st)
    y_p = xp.reshape(bp, tp, D_MODEL)
    y_s = xs.reshape(ts, bs, D_MODEL).transpose(1, 0, 2)
    stack = lambda lst, i: jnp.stack([s[i] for s in lst])
    return (y_p, y_s,
            stack(sp, 0), stack(sp, 1), stack(sp, 2), stack(sp, 3), stack(sp, 4), stack(sp, 5),
            stack(sp, 6), stack(sp, 7),
            stack(ss, 0), stack(ss, 1), stack(ss, 2), stack(ss, 3), stack(ss, 4), stack(ss, 5))
```

```python
import functools
import math

import numpy as np
import jax
import jax.numpy as jnp
from jax import lax
from jax.experimental import pallas as pl
from jax.experimental.pallas import tpu as pltpu

F32 = jnp.float32
BF16 = jnp.bfloat16

D_MODEL = 1024
PAST_LEN = 16384
SWA_HEADS = 8
SWA_KV_HEADS = 2
SWA_HEAD_DIM = 64
SWA_WINDOW = 128
ROPE_DIM = 16
ROPE_THETA = 500000.0
LRU_WIDTH = 512
LRU_BLOCKS = 8
LRU_CONV_W = 4
LRU_C = 8.0
GLA_HEADS = 4
GLA_DK = 64
GLA_DV = 128
GLA_RANK = 16
GLA_TAU = 16.0
GLA_CHUNK = 32
MEM_HEADS = 4
MEM_HEAD_DIM = 128
D_FF = 2816
FFN_CONV_W = 3
EPS = 1e-6
NEG = -1e30

SWA_Q = SWA_HEADS * SWA_HEAD_DIM
SWA_KV = SWA_KV_HEADS * SWA_HEAD_DIM
GLA_K = GLA_HEADS * GLA_DK
GLA_V = GLA_HEADS * GLA_DV
MEM_W = MEM_HEADS * MEM_HEAD_DIM

LANES = 128
VMEM_LIMIT = 56 * 1024 * 1024

C_Q = 0
C_KV = 512
C_XB = 768
C_YB = 1280
C_GQK = 1792
C_GV = 2304
C_GR = 2816
C_AC = 3328
C_GATE = 3456
IN_COLS_PAD = 6528
AC_PAD = LANES - GLA_RANK


def _cparams(sem, vmem=VMEM_LIMIT):
    return pltpu.CompilerParams(dimension_semantics=sem, vmem_limit_bytes=vmem)


def _full(shape):
    n = len(shape)
    return pl.BlockSpec(shape, lambda *_: (0,) * n)


def _rows(tm, c):
    return pl.BlockSpec((tm, c), lambda i: (i, 0))


def _rms(x, g):
    ms = jnp.mean(x * x, axis=-1, keepdims=True)
    return x * lax.rsqrt(ms + EPS) * g


def _sigmoid(x):
    return 1.0 / (1.0 + jnp.exp(-x))


def _silu(x):
    return x * _sigmoid(x)


def _gelu_tanh(x):
    return 0.5 * x * (1.0 + jnp.tanh(math.sqrt(2.0 / math.pi) * (x + 0.044715 * (x * x * x))))


def _softplus(z):
    return jnp.maximum(z, 0.0) + jnp.log1p(jnp.exp(-jnp.abs(z)))


def _neg_expm1(x):
    p = x * (1.0 / 120.0) + (1.0 / 24.0)
    for coef in (1.0 / 6.0, 0.5, 1.0):
        p = p * x + coef
    return jnp.where(jnp.abs(x) < 0.125, -(x * p), 1.0 - jnp.exp(x))


def _block_ones(n, blk_r, blk_c=None):
    blk_c = blk_r if blk_c is None else blk_c
    m = np.kron(np.eye(n, dtype=np.float32), np.ones((blk_r, blk_c), np.float32))
    return jnp.asarray(m, dtype=BF16)


def _shift_rows(x, d, row):
    del row
    return pltpu.roll(x, d, 0)


def _rope(x, rc, rs1, rs2):
    return x * rc + pltpu.roll(x, LANES - 8, 1) * rs1 + pltpu.roll(x, 8, 1) * rs2


def _in_proj_kernel(x_ref, g_ref, w_ref, e_ref, qg_ref, kg_ref, rc_ref, rs1_ref, rs2_ref,
                    wa2_ref, ba_ref,
                    q_ref, k_ref, v_ref, xb_ref, gy_ref, gq_ref, gk_ref, gv_ref, sr_ref, la_ref,
                    gate_ref):
    h = _rms(x_ref[...], g_ref[...]).astype(BF16)

    def mm(c0, c1):
        return jnp.dot(h, w_ref[:, c0:c1], preferred_element_type=F32)

    rc, rs1, rs2 = rc_ref[...], rs1_ref[...], rs2_ref[...]
    inv_hd = 1.0 / SWA_HEAD_DIM

    qf = mm(C_Q, C_Q + SWA_Q)
    ss = jnp.dot((qf * qf).astype(BF16), e_ref[...], preferred_element_type=F32)
    qn = qf * lax.rsqrt(ss * inv_hd + EPS)
    for p in range(SWA_Q // LANES):
        qp = qn[:, p * LANES:(p + 1) * LANES] * qg_ref[...]
        q_ref[p] = (_rope(qp, rc, rs1, rs2) * (SWA_HEAD_DIM ** -0.5)).astype(BF16)

    kv = mm(C_KV, C_KV + 2 * SWA_KV)
    kf = kv[:, :SWA_KV]
    ssk = jnp.dot((kf * kf).astype(BF16), e_ref[0:SWA_KV, 0:SWA_KV], preferred_element_type=F32)
    kn = kf * lax.rsqrt(ssk * inv_hd + EPS) * kg_ref[...]
    k_ref[...] = _rope(kn, rc, rs1, rs2)
    v_ref[...] = kv[:, SWA_KV:]

    xb_ref[...] = mm(C_XB, C_XB + LRU_WIDTH).astype(BF16)
    gy_ref[...] = _gelu_tanh(mm(C_YB, C_YB + LRU_WIDTH)).astype(BF16)

    qk = mm(C_GQK, C_GQK + 2 * GLA_K)
    gq_ref[...] = qk[:, :GLA_K] * (GLA_DK ** -0.5)
    gk_ref[...] = qk[:, GLA_K:]
    gv_ref[...] = mm(C_GV, C_GV + GLA_V).astype(BF16)
    sr_ref[...] = _silu(mm(C_GR, C_GR + GLA_V)).astype(BF16)

    ac = mm(C_AC, C_AC + LANES).astype(BF16)
    z = jnp.dot(ac, wa2_ref[...], preferred_element_type=F32) + ba_ref[...]
    la_ref[...] = -_softplus(-z) * (1.0 / GLA_TAU)

    for c in range(3 * D_MODEL // 512):
        gate_ref[:, c * 512:(c + 1) * 512] = _sigmoid(
            mm(C_GATE + c * 512, C_GATE + (c + 1) * 512)).astype(BF16)


def _in_proj(x, w, rope_tabs, tm):
    m = x.shape[0]
    rc, rs1, rs2 = rope_tabs
    ntab = rc.shape[0] // tm
    tab_spec = pl.BlockSpec((tm, LANES), lambda i: (i % ntab, 0))
    outs = [
        jax.ShapeDtypeStruct((SWA_Q // LANES, m, LANES), BF16),
        jax.ShapeDtypeStruct((m, SWA_KV), F32),
        jax.ShapeDtypeStruct((m, SWA_KV), F32),
        jax.ShapeDtypeStruct((m, LRU_WIDTH), BF16),
        jax.ShapeDtypeStruct((m, LRU_WIDTH), BF16),
        jax.ShapeDtypeStruct((m, GLA_K), F32),
        jax.ShapeDtypeStruct((m, GLA_K), F32),
        jax.ShapeDtypeStruct((m, GLA_V), BF16),
        jax.ShapeDtypeStruct((m, GLA_V), BF16),
        jax.ShapeDtypeStruct((m, GLA_K), F32),
        jax.ShapeDtypeStruct((m, 3 * D_MODEL), BF16),
    ]
    out_specs = [
        pl.BlockSpec((SWA_Q // LANES, tm, LANES), lambda i: (0, i, 0)),
        _rows(tm, SWA_KV), _rows(tm, SWA_KV), _rows(tm, LRU_WIDTH), _rows(tm, LRU_WIDTH),
        _rows(tm, GLA_K), _rows(tm, GLA_K), _rows(tm, GLA_V), _rows(tm, GLA_V), _rows(tm, GLA_K),
        _rows(tm, 3 * D_MODEL),
    ]
    in_specs = [
        _rows(tm, D_MODEL), _full((1, D_MODEL)), _full((D_MODEL, IN_COLS_PAD)),
        _full((SWA_Q, SWA_Q)), _full((1, LANES)), _full((1, LANES)),
        tab_spec, tab_spec, tab_spec,
        _full((LANES, GLA_K)), _full((1, GLA_K)),
    ]
    return pl.pallas_call(
        _in_proj_kernel, grid=(m // tm,), in_specs=in_specs, out_specs=out_specs, out_shape=outs,
        compiler_params=_cparams(("parallel",)), name="in_proj",
    )(x, w["norm_mix_g"], w["w_in"], w["e_swa"], w["swa_qn_g"], w["swa_kn_g"], rc, rs1, rs2,
      w["gla_wa2"], w["gla_ba"])


def _sink_softmax(s, sk):
    m = jnp.maximum(jnp.max(s, axis=-1, keepdims=True), sk)
    e = jnp.exp(s - m)
    den = jnp.sum(e, axis=-1, keepdims=True) + jnp.exp(sk - m)
    return e / den


def _swa_prompt_kernel(q_ref, kc_ref, kp_ref, vc_ref, vp_ref, sink_ref, o_ref, *, nblk):
    j = pl.program_id(1)
    w = SWA_WINDOW
    lane = lax.broadcasted_iota(jnp.int32, (1, LANES), 1)
    lo = lane < SWA_HEAD_DIM

    def dup(x, kvh):
        r = pltpu.roll(x, SWA_HEAD_DIM, 1)
        y = jnp.where(lo, x, r) if kvh == 0 else jnp.where(lo, r, x)
        return y.astype(BF16)

    kall = jnp.concatenate([kp_ref[...], kc_ref[...]], axis=0)
    vall = jnp.concatenate([vp_ref[...], vc_ref[...]], axis=0)
    kd = [dup(kall, 0), dup(kall, 1)]
    vd = [dup(vall, 0), dup(vall, 1)]

    qi = lax.broadcasted_iota(jnp.int32, (w, 2 * w), 0) + w
    ki = lax.broadcasted_iota(jnp.int32, (w, 2 * w), 1)
    rel = qi - ki
    band = (rel >= 0) & (rel <= w)
    first = band & ((ki >= w) | ((jnp.zeros_like(ki) + j) > 0))
    row = lax.broadcasted_iota(jnp.int32, (2 * w, 1), 0)

    for c in range(nblk):
        ok = first if c == 0 else band
        ok2 = jnp.concatenate([ok, ok], axis=0)
        for kvh in range(SWA_KV_HEADS):
            kk = kd[kvh][c * w:(c + 2) * w]
            vv = vd[kvh][c * w:(c + 2) * w]
            qg = q_ref[2 * kvh:2 * kvh + 2, c * w:(c + 1) * w, :].reshape(2 * w, LANES)
            res = []
            for par in range(2):
                qm = jnp.where(lo if par == 0 else jnp.logical_not(lo), qg, jnp.zeros_like(qg))
                s = lax.dot_general(qm, kk, (((1,), (1,)), ((), ())), preferred_element_type=F32)
                s = jnp.where(ok2, s, NEG)
                h0 = kvh * 4 + par
                sk = jnp.where(row < w, sink_ref[h0], sink_ref[h0 + 2])
                p = _sink_softmax(s, sk).astype(BF16)
                res.append(jnp.dot(p, vv, preferred_element_type=F32))
            oo = jnp.where(lo, res[0], res[1]).astype(BF16)
            o_ref[2 * kvh:2 * kvh + 2, c * w:(c + 1) * w, :] = oo.reshape(2, w, LANES)


def _swa_prompt(q4, k, v, sink, b, t):
    m = b * t
    w = SWA_WINDOW
    tq = min(512, t)
    nj = t // tq
    nblk = tq // w
    r = t // w
    cur = lambda bb, j: (bb * nj + j, 0)
    prev = lambda bb, j: (jnp.maximum(bb * r + j * nblk - 1, 0), 0)
    qspec = pl.BlockSpec((4, tq, LANES), lambda bb, j: (0, bb * nj + j, 0))
    return pl.pallas_call(
        functools.partial(_swa_prompt_kernel, nblk=nblk),
        grid=(b, nj),
        in_specs=[qspec, pl.BlockSpec((tq, LANES), cur), pl.BlockSpec((w, LANES), prev),
                  pl.BlockSpec((tq, LANES), cur), pl.BlockSpec((w, LANES), prev),
                  pl.BlockSpec(memory_space=pltpu.SMEM)],
        out_specs=qspec,
        out_shape=jax.ShapeDtypeStruct((4, m, LANES), BF16),
        compiler_params=_cparams(("parallel", "parallel")), name="swa_prompt",
    )(q4, k, k, v, v, sink)


def _swa_step_kernel(q_ref, kc_ref, kn_ref, vc_ref, vn_ref, sk_ref, o_ref, *, bb, ts):
    w = SWA_WINDOW
    nk = w + 8
    row_t = lax.broadcasted_iota(jnp.int32, (SWA_HEADS * ts, nk), 0) % ts
    key = lax.broadcasted_iota(jnp.int32, (SWA_HEADS * ts, nk), 1)
    ok = ((key < w) & (key >= row_t)) | ((key >= w) & ((key - w) <= row_t))
    sk = sk_ref[...]
    for i in range(bb):
        kk = jnp.concatenate([kc_ref[i], kn_ref[i]], axis=0).astype(BF16)
        vv = jnp.concatenate([vc_ref[i], vn_ref[i]], axis=0).astype(BF16)
        s = lax.dot_general(q_ref[i], kk, (((1,), (1,)), ((), ())), preferred_element_type=F32)
        s = jnp.where(ok, s, NEG)
        p = _sink_softmax(s, sk).astype(BF16)
        o_ref[i] = jnp.dot(p, vv, preferred_element_type=F32)


def _swa_step(qm, kc, kn8, vc, vn8, sink_rows, ts):
    b = qm.shape[0]
    bb = 8
    nq = SWA_HEADS * ts
    w = SWA_WINDOW
    blk = lambda r: pl.BlockSpec((bb, r, LANES), lambda i: (i, 0, 0))
    return pl.pallas_call(
        functools.partial(_swa_step_kernel, bb=bb, ts=ts),
        grid=(b // bb,),
        in_specs=[blk(nq), blk(w), blk(8), blk(w), blk(8), _full((nq, 1))],
        out_specs=blk(nq),
        out_shape=jax.ShapeDtypeStruct((b, nq, LANES), F32),
        compiler_params=_cparams(("parallel",)), name="swa_step",
    )(qm, kc, kn8, vc, vn8, sink_rows)


def _hist_rows(s, kw):
    return 8 if s == 1 else (kw - 1) * s


def _conv_taps(x, hist, cw_rows, bias, *, s, tm, kw):
    y = bias + x * cw_rows[kw - 1:kw]
    row8 = lax.broadcasted_iota(jnp.int32, (8, 1), 0)
    for k in range(1, kw):
        d = k * s
        if s == 1:
            sh = pltpu.roll(x, d, 0)
            top = jnp.where(row8 < d, pltpu.roll(hist, d, 0), sh[:8])
            sh = jnp.concatenate([top, sh[8:]], axis=0)
        else:
            sh = jnp.concatenate([hist[hist.shape[0] - d:], x[:tm - d]], axis=0)
        y = y + sh * cw_rows[kw - 1 - k:kw - k]
    return y, x[tm - _hist_rows(s, kw):]


def _combine_shifted(a, b, d, keep, axis):
    a_sh = jnp.where(keep, pltpu.roll(a, d, axis), 1.0)
    b_sh = jnp.where(keep, pltpu.roll(b, d, axis), 0.0)
    return a * a_sh, a * b_sh + b


def _scan_adjacent_rows(a, b, hprev, a_sc, b_sc, hin_sc, tm):
    c = a.shape[1]
    ng = tm // 8
    nl = c // LANES
    a3 = a.reshape(ng, 8, c)
    b3 = b.reshape(ng, 8, c)
    r8 = lax.broadcasted_iota(jnp.int32, (1, 8, 1), 1)
    d = 1
    while d < 8:
        a3, b3 = _combine_shifted(a3, b3, d, r8 >= d, 1)
        d *= 2
    a2 = a3.reshape(tm, c)
    b2 = b3.reshape(tm, c)
    for l in range(nl):
        a_sc[l] = a2[:, l * LANES:(l + 1) * LANES]
        b_sc[l] = b2[:, l * LANES:(l + 1) * LANES]
    last = pl.ds(7, ng, stride=8)
    ag = jnp.concatenate([a_sc[l, last, :] for l in range(nl)], axis=1)
    bg = jnp.concatenate([b_sc[l, last, :] for l in range(nl)], axis=1)
    rg = lax.broadcasted_iota(jnp.int32, (ng, 1), 0)
    d = 1
    while d < ng:
        ag, bg = _combine_shifted(ag, bg, d, rg >= d, 0)
        d *= 2
    hg = ag * hprev + bg
    hin_sc[...] = jnp.where(rg >= 1, pltpu.roll(hg, 1, 0), hprev)
    return jnp.concatenate(
        [a2[g * 8:(g + 1) * 8] * hin_sc[g:g + 1, :] + b2[g * 8:(g + 1) * 8] for g in range(ng)],
        axis=0)


def _lru_kernel(xb_ref, gy_ref, hist_ref, h0_ref, cw_ref, cb_ref, wa_ref, ba_ref, wx_ref, bx_ref,
                lam_ref, ob_ref, hl_ref, co_ref, cx_sc, ch_sc, *scan_sc, s, tm):
    j = pl.program_id(1)
    kw = LRU_CONV_W

    @pl.when(j == 0)
    def _():
        cx_sc[...] = hist_ref[...]
        ch_sc[...] = h0_ref[...]

    x = xb_ref[...].astype(F32)
    xc, nh = _conv_taps(x, cx_sc[...], cw_ref[...], cb_ref[...], s=s, tm=tm, kw=kw)
    cx_sc[...] = nh
    co_ref[...] = nh

    xcb = xc.astype(BF16)
    r_g = _sigmoid(jnp.dot(xcb, wa_ref[...], preferred_element_type=F32) + ba_ref[...])
    i_g = _sigmoid(jnp.dot(xcb, wx_ref[...], preferred_element_type=F32) + bx_ref[...])
    decay = (-LRU_C) * _softplus(-lam_ref[...])
    a = jnp.exp(r_g * decay)
    bv = jnp.sqrt(_neg_expm1(r_g * (2.0 * decay))) * (i_g * xc)

    if s == 1:
        h = _scan_adjacent_rows(a, bv, ch_sc[...], *scan_sc, tm)
        hlast = h[tm - 8:, :][7:, :]
    else:
        row = lax.broadcasted_iota(jnp.int32, (tm, 1), 0)
        h = bv + a * jnp.concatenate([ch_sc[...], jnp.zeros((tm - s, LRU_WIDTH), F32)], axis=0)
        d = s
        while d < tm:
            a, h = _combine_shifted(a, h, d, row >= d, 0)
            d *= 2
        hlast = h[tm - s:, :]
    ch_sc[...] = hlast
    hl_ref[...] = hlast
    ob_ref[...] = (gy_ref[...].astype(F32) * h).astype(BF16)


def _lru(xb, gy, hist, h0, w, s, tm, ngroups):
    m = xb.shape[0]
    nj = m // (ngroups * tm)
    kw = LRU_CONV_W
    hr = _hist_rows(s, kw)
    rowspec = pl.BlockSpec((tm, LRU_WIDTH), lambda g, j: (g * nj + j, 0))
    gspec = lambda r: pl.BlockSpec((None, r, LRU_WIDTH), lambda g, j: (g, 0, 0))
    full2 = lambda shape: pl.BlockSpec(shape, lambda g, j: (0, 0))
    scratch = [pltpu.VMEM((hr, LRU_WIDTH), F32), pltpu.VMEM((s, LRU_WIDTH), F32)]
    if s == 1:
        scratch += [pltpu.VMEM((LRU_WIDTH // LANES, tm, LANES), F32),
                    pltpu.VMEM((LRU_WIDTH // LANES, tm, LANES), F32),
                    pltpu.VMEM((tm // 8, LRU_WIDTH), F32)]
    return pl.pallas_call(
        functools.partial(_lru_kernel, s=s, tm=tm),
        grid=(ngroups, nj),
        in_specs=[rowspec, rowspec, gspec(hr), gspec(s),
                  full2((kw, LRU_WIDTH)), full2((1, LRU_WIDTH)),
                  full2((LRU_WIDTH, LRU_WIDTH)), full2((1, LRU_WIDTH)),
                  full2((LRU_WIDTH, LRU_WIDTH)), full2((1, LRU_WIDTH)), full2((1, LRU_WIDTH))],
        out_specs=[rowspec, gspec(s), gspec(hr)],
        out_shape=[jax.ShapeDtypeStruct((m, LRU_WIDTH), BF16),
                   jax.ShapeDtypeStruct((ngroups, s, LRU_WIDTH), F32),
                   jax.ShapeDtypeStruct((ngroups, hr, LRU_WIDTH), F32)],
        scratch_shapes=scratch,
        compiler_params=_cparams(("arbitrary", "arbitrary")), name="lru",
    )(xb, gy, hist, h0, w["lru_conv_w"], w["lru_conv_b"], w["lru_wa"], w["lru_ba"], w["lru_wx"],
      w["lru_bx"], w["lru_lambda"])


def _gla_prompt_kernel(q_ref, k_ref, v_ref, sr_ref, la_ref, g_ref, o_ref, st_ref, s_sc, *, tm):
    j = pl.program_id(1)
    c32 = GLA_CHUNK
    nc = tm // c32

    @pl.when(j == 0)
    def _():
        s_sc[...] = jnp.zeros_like(s_sc)

    row = lax.broadcasted_iota(jnp.int32, (tm, 1), 0)
    rin = row % c32
    cum = la_ref[...]
    d = 1
    while d < c32:
        cum = cum + jnp.where(rin >= d, pltpu.roll(cum, d, 0), 0.0)
        d *= 2
    cum3 = cum.reshape(nc, c32, GLA_K)
    last3 = cum3[:, c32 - 1:c32, :]
    k_all = k_ref[...]
    qt = q_ref[...] * jnp.exp(cum)
    kt = (k_all * jnp.exp(-cum)).astype(BF16)
    kl = (k_all.reshape(nc, c32, GLA_K) * jnp.exp(last3 - cum3)).reshape(tm, GLA_K)
    dec = jnp.exp(last3)

    ti = lax.broadcasted_iota(jnp.int32, (tm, tm), 0)
    si = lax.broadcasted_iota(jnp.int32, (tm, tm), 1)
    causal = (si <= ti) & ((ti // c32) == (si // c32))
    lane = lax.broadcasted_iota(jnp.int32, (1, LANES), 1)
    lo = lane < GLA_DK
    rch = row // c32

    def by_chunk(x):
        return jnp.concatenate([jnp.where(rch == c, x, 0.0).astype(BF16) for c in range(nc)], axis=1)

    for p in range(GLA_HEADS // 2):
        sl = slice(p * LANES, (p + 1) * LANES)
        qtp = qt[:, sl]
        ktp = kt[:, sl]
        kl_x = by_chunk(kl[:, sl])
        for par in range(2):
            h = 2 * p + par
            hs = slice(h * GLA_DV, (h + 1) * GLA_DV)
            qmf = jnp.where(lo if par == 0 else jnp.logical_not(lo), qtp, 0.0)
            qm = qmf.astype(BF16)
            att = lax.dot_general(qm, ktp, (((1,), (1,)), ((), ())), preferred_element_type=F32)
            att = jnp.where(causal, att, 0.0).astype(BF16)
            vh = v_ref[:, hs]
            o_intra = jnp.dot(att, vh, preferred_element_type=F32)
            ut = lax.dot_general(vh, kl_x, (((0,), (0,)), ((), ())), preferred_element_type=F32)
            st = s_sc[h]
            sts = []
            for c in range(nc):
                sts.append(st.astype(BF16))
                st = st * dec[c][:, sl] + ut[:, c * LANES:(c + 1) * LANES]
            s_sc[h] = st
            o_inter = lax.dot_general(by_chunk(qmf), jnp.concatenate(sts, axis=1),
                                      (((1,), (1,)), ((), ())), preferred_element_type=F32)
            o = o_intra + o_inter
            on = _rms(o, g_ref[...])
            o_ref[:, hs] = (on * sr_ref[:, hs].astype(F32)).astype(BF16)
    st_ref[...] = s_sc[...]


def _gla_prompt(gq, gk, gv, sr, la, on_g, b, t):
    m = b * t
    tm = min(256, t)
    nj = t // tm
    rs = lambda c: pl.BlockSpec((tm, c), lambda bb, j: (bb * nj + j, 0))
    sspec = pl.BlockSpec((None, GLA_HEADS, GLA_DV, LANES), lambda bb, j: (bb, 0, 0, 0))
    return pl.pallas_call(
        functools.partial(_gla_prompt_kernel, tm=tm),
        grid=(b, nj),
        in_specs=[rs(GLA_K), rs(GLA_K), rs(GLA_V), rs(GLA_V), rs(GLA_K),
                  pl.BlockSpec((1, GLA_DV), lambda bb, j: (0, 0))],
        out_specs=[rs(GLA_V), sspec],
        out_shape=[jax.ShapeDtypeStruct((m, GLA_V), BF16),
                   jax.ShapeDtypeStruct((b, GLA_HEADS, GLA_DV, LANES), F32)],
        scratch_shapes=[pltpu.VMEM((GLA_HEADS, GLA_DV, LANES), F32)],
        compiler_params=_cparams(("arbitrary", "arbitrary")), name="gla_prompt",
    )(gq, gk, gv, sr, la, on_g)


def _gla_step_kernel(q_ref, k_ref, v_ref, sr_ref, la_ref, s0_ref, g_ref, o_ref, sn_ref, *, bb, ts):
    row = lax.broadcasted_iota(jnp.int32, (8, 1), 0)
    ti = lax.broadcasted_iota(jnp.int32, (8, 8), 0)
    si = lax.broadcasted_iota(jnp.int32, (8, 8), 1)
    causal = si <= ti
    lane = lax.broadcasted_iota(jnp.int32, (1, LANES), 1)
    lo = lane < GLA_DK
    for i in range(bb):
        cum = la_ref[i]
        d = 1
        while d < ts:
            cum = cum + jnp.where(row >= d, pltpu.roll(cum, d, 0), 0.0)
            d *= 2
        last = cum[ts - 1:ts, :]
        k_all = k_ref[i]
        qt = q_ref[i] * jnp.exp(cum)
        kt = (k_all * jnp.exp(-cum)).astype(BF16)
        kl = (k_all * jnp.exp(last - cum)).astype(BF16)
        dec = jnp.exp(last)
        for p in range(GLA_HEADS // 2):
            sl = slice(p * LANES, (p + 1) * LANES)
            sp = s0_ref[i, p]
            dcol = jnp.broadcast_to(dec[:, sl], (LANES, LANES)).T
            spb = sp.astype(BF16)
            klp = kl[:, sl]
            new_rows = []
            for par in range(2):
                h = 2 * p + par
                hs = slice(h * GLA_DV, (h + 1) * GLA_DV)
                qm = jnp.where(lo if par == 0 else jnp.logical_not(lo), qt[:, sl], 0.0).astype(BF16)
                att = lax.dot_general(qm, kt[:, sl], (((1,), (1,)), ((), ())),
                                      preferred_element_type=F32)
                att = jnp.where(causal, att, 0.0).astype(BF16)
                vh = v_ref[i][:, hs]
                o = jnp.dot(att, vh, preferred_element_type=F32) + jnp.dot(
                    qm, spb, preferred_element_type=F32)
                on = _rms(o, g_ref[...])
                o_ref[i, :, hs] = (on * sr_ref[i][:, hs].astype(F32)).astype(BF16)
                u = lax.dot_general(klp, vh, (((0,), (0,)), ((), ())), preferred_element_type=F32)
                new_rows.append(u[par * GLA_DK:(par + 1) * GLA_DK, :])
            sn_ref[i, p] = sp * dcol + jnp.concatenate(new_rows, axis=0)


def _gla_step(gq8, gk8, gv8, sr8, la8, s0p, on_g, ts):
    b = gq8.shape[0]
    bb = 8
    blk = lambda c: pl.BlockSpec((bb, 8, c), lambda i: (i, 0, 0))
    sspec = pl.BlockSpec((bb, GLA_HEADS // 2, LANES, GLA_DV), lambda i: (i, 0, 0, 0))
    return pl.pallas_call(
        functools.partial(_gla_step_kernel, bb=bb, ts=ts),
        grid=(b // bb,),
        in_specs=[blk(GLA_K), blk(GLA_K), blk(GLA_V), blk(GLA_V), blk(GLA_K), sspec,
                  _full((1, GLA_DV))],
        out_specs=[blk(GLA_V), sspec],
        out_shape=[jax.ShapeDtypeStruct((b, 8, GLA_V), BF16),
                   jax.ShapeDtypeStruct((b, GLA_HEADS // 2, LANES, GLA_DV), F32)],
        compiler_params=_cparams(("parallel",)), name="gla_step",
    )(gq8, gk8, gv8, sr8, la8, s0p, on_g)


def _merge_kernel(oa_ref, ob_ref, oc_ref, gate_ref, x_ref, wa_ref, wb_ref, wc_ref, wo_ref,
                  nx_ref, wq_ref, qn_ref, x1_ref, qx_ref):
    oa = jnp.concatenate([oa_ref[p] for p in range(SWA_Q // LANES)], axis=-1)
    dm = D_MODEL
    merged = (gate_ref[:, 0:dm].astype(F32) * jnp.dot(oa, wa_ref[...], preferred_element_type=F32)
              + gate_ref[:, dm:2 * dm].astype(F32)
              * jnp.dot(ob_ref[...], wb_ref[...], preferred_element_type=F32)
              + gate_ref[:, 2 * dm:3 * dm].astype(F32)
              * jnp.dot(oc_ref[...], wc_ref[...], preferred_element_type=F32))
    x1 = x_ref[...] + jnp.dot(merged.astype(BF16), wo_ref[...], preferred_element_type=F32)
    x1_ref[...] = x1
    hx = _rms(x1, nx_ref[...]).astype(BF16)
    q = jnp.dot(hx, wq_ref[...], preferred_element_type=F32)
    for h in range(MEM_HEADS):
        hs = slice(h * MEM_HEAD_DIM, (h + 1) * MEM_HEAD_DIM)
        qx_ref[:, hs] = _rms(q[:, hs], qn_ref[...]).astype(BF16)


def _merge(oa4, ob, oc, gates, x, w, tm):
    m = x.shape[0]
    return pl.pallas_call(
        _merge_kernel, grid=(m // tm,),
        in_specs=[pl.BlockSpec((SWA_Q // LANES, tm, LANES), lambda i: (0, i, 0)),
                  _rows(tm, LRU_WIDTH), _rows(tm, GLA_V), _rows(tm, 3 * D_MODEL), _rows(tm, D_MODEL),
                  _full((SWA_Q, D_MODEL)), _full((LRU_WIDTH, D_MODEL)), _full((GLA_V, D_MODEL)),
                  _full((D_MODEL, D_MODEL)), _full((1, D_MODEL)), _full((D_MODEL, MEM_W)),
                  _full((1, MEM_HEAD_DIM))],
        out_specs=[_rows(tm, D_MODEL), _rows(tm, MEM_W)],
        out_shape=[jax.ShapeDtypeStruct((m, D_MODEL), F32), jax.ShapeDtypeStruct((m, MEM_W), BF16)],
        compiler_params=_cparams(("parallel",)), name="merge",
    )(oa4, ob, oc, gates, x, w["w_branch_a"], w["w_branch_b"], w["w_branch_c"], w["w_out"],
      w["norm_x_g"], w["x_wq"], w["x_qn_g"])


def _memkv_kernel(m_ref, g_ref, wk_ref, wv_ref, kn_ref, k_ref, v_ref):
    hm = _rms(m_ref[...], g_ref[...]).astype(BF16)
    k = jnp.dot(hm, wk_ref[...], preferred_element_type=F32)
    for h in range(MEM_HEADS):
        hs = slice(h * MEM_HEAD_DIM, (h + 1) * MEM_HEAD_DIM)
        k_ref[:, hs] = _rms(k[:, hs], kn_ref[...])
    v_ref[...] = jnp.dot(hm, wv_ref[...], preferred_element_type=F32)


def _memkv(mem, w, tm):
    m = mem.shape[0]
    return pl.pallas_call(
        _memkv_kernel, grid=(m // tm,),
        in_specs=[_rows(tm, D_MODEL), _full((1, D_MODEL)), _full((D_MODEL, MEM_W)),
                  _full((D_MODEL, MEM_W)), _full((1, MEM_HEAD_DIM))],
        out_specs=[_rows(tm, MEM_W), _rows(tm, MEM_W)],
        out_shape=[jax.ShapeDtypeStruct((m, MEM_W), F32), jax.ShapeDtypeStruct((m, MEM_W), F32)],
        compiler_params=_cparams(("parallel",)), name="memkv",
    )(mem, w["norm_mem_g"], w["x_wk"], w["x_wv"], w["x_kn_g"])


def _xattn_heads(q, mk, mv):
    outs = []
    for h in range(MEM_HEADS):
        hs = slice(h * MEM_HEAD_DIM, (h + 1) * MEM_HEAD_DIM)
        s = lax.dot_general(q[:, hs], mk[:, hs].astype(BF16), (((1,), (1,)), ((), ())),
                            preferred_element_type=F32) * (MEM_HEAD_DIM ** -0.5)
        m = jnp.max(s, axis=-1, keepdims=True)
        e = jnp.exp(s - m)
        p = (e / jnp.sum(e, axis=-1, keepdims=True)).astype(BF16)
        outs.append(jnp.dot(p, mv[:, hs].astype(BF16), preferred_element_type=F32))
    return jnp.concatenate(outs, axis=-1)


def _xattn_prompt_kernel(q_ref, mk_ref, mv_ref, o_ref):
    o_ref[...] = _xattn_heads(q_ref[...], mk_ref[...], mv_ref[...]).astype(BF16)


def _xattn_prompt(qx, mk, mv, b, t):
    m = b * t
    n_mem = mk.shape[0] // b
    tq = min(512, t)
    nj = t // tq
    qspec = pl.BlockSpec((tq, MEM_W), lambda bb, j: (bb * nj + j, 0))
    mspec = pl.BlockSpec((n_mem, MEM_W), lambda bb, j: (bb, 0))
    return pl.pallas_call(
        _xattn_prompt_kernel, grid=(b, nj), in_specs=[qspec, mspec, mspec], out_specs=qspec,
        out_shape=jax.ShapeDtypeStruct((m, MEM_W), BF16),
        compiler_params=_cparams(("parallel", "parallel")), name="xattn_prompt",
    )(qx, mk, mv)


def _xattn_step_kernel(q_ref, mk_ref, mv_ref, o_ref, *, bb):
    for i in range(bb):
        s = lax.dot_general(q_ref[i], mk_ref[i].astype(BF16), (((1,), (1,)), ((), ())),
                            preferred_element_type=F32) * (MEM_HEAD_DIM ** -0.5)
        m = jnp.max(s, axis=-1, keepdims=True)
        e = jnp.exp(s - m)
        p = (e / jnp.sum(e, axis=-1, keepdims=True)).astype(BF16)
        o_ref[i] = jnp.dot(p, mv_ref[i].astype(BF16), preferred_element_type=F32)


def _xattn_step(qexp, mk_all, mv_all, layer):
    b, nq = qexp.shape[0], qexp.shape[1]
    n_mem = mk_all.shape[2]
    bb = 8
    qspec = pl.BlockSpec((bb, nq, MEM_W), lambda i: (i, 0, 0))
    mspec = pl.BlockSpec((None, bb, n_mem, MEM_W), lambda i: (layer, i, 0, 0))
    return pl.pallas_call(
        functools.partial(_xattn_step_kernel, bb=bb), grid=(b // bb,),
        in_specs=[qspec, mspec, mspec], out_specs=qspec,
        out_shape=jax.ShapeDtypeStruct((b, nq, MEM_W), F32),
        compiler_params=_cparams(("parallel",)), name="xattn_step",
    )(qexp, mk_all, mv_all)


def _ffn_kernel(ox_ref, x_ref, hist_ref, wo_ref, g_ref, wu_ref, cw_ref, cb_ref, wd_ref,
                x3_ref, co_ref, act_sc, *hs, s, tm):
    kw = FFN_CONV_W
    cwid = 256
    if hs:
        hs_sc = hs[0]

        @pl.when(pl.program_id(1) == 0)
        def _():
            hs_sc[...] = hist_ref[...]
    else:
        hs_sc = hist_ref

    x2 = x_ref[...] + jnp.dot(ox_ref[...], wo_ref[...], preferred_element_type=F32)
    x3_ref[...] = x2
    hf = _rms(x2, g_ref[...]).astype(BF16)
    for c in range(D_FF // cwid):
        ys = []
        for c0 in (c * cwid, D_FF + c * cwid):
            cs = slice(c0, c0 + cwid)
            u = jnp.dot(hf, wu_ref[:, cs], preferred_element_type=F32)
            y, nh = _conv_taps(u, hs_sc[:, cs], cw_ref[:, cs], cb_ref[:, cs], s=s, tm=tm, kw=kw)
            if hs:
                hs_sc[:, cs] = nh
            co_ref[:, cs] = nh
            ys.append(y)
        act_sc[:, c * cwid:(c + 1) * cwid] = (_silu(ys[0]) * ys[1]).astype(BF16)
    x3_ref[...] = x3_ref[...] + jnp.dot(act_sc[...], wd_ref[...], preferred_element_type=F32)


def _ffn(ox, x1, hist, w, s, tm, ngroups):
    m = x1.shape[0]
    nj = m // (ngroups * tm)
    hr = _hist_rows(s, FFN_CONV_W)
    rs = lambda c: pl.BlockSpec((tm, c), lambda g, j: (g * nj + j, 0))
    hspec = pl.BlockSpec((None, hr, 2 * D_FF), lambda g, j: (g, 0, 0), pipeline_mode=pl.Buffered(1))
    ospec = pl.BlockSpec((None, hr, 2 * D_FF), lambda g, j: (g, 0, 0))
    wfull = lambda shape: pl.BlockSpec(shape, lambda g, j: (0, 0), pipeline_mode=pl.Buffered(1))
    scratch = [pltpu.VMEM((tm, D_FF), BF16)]
    if nj > 1:
        scratch.append(pltpu.VMEM((hr, 2 * D_FF), F32))
    return pl.pallas_call(
        functools.partial(_ffn_kernel, s=s, tm=tm),
        grid=(ngroups, nj),
        in_specs=[rs(MEM_W), rs(D_MODEL), hspec, wfull((MEM_W, D_MODEL)), wfull((1, D_MODEL)),
                  wfull((D_MODEL, 2 * D_FF)), wfull((FFN_CONV_W, 2 * D_FF)), wfull((1, 2 * D_FF)),
                  wfull((D_FF, D_MODEL))],
        out_specs=[rs(D_MODEL), ospec],
        out_shape=[jax.ShapeDtypeStruct((m, D_MODEL), F32),
                   jax.ShapeDtypeStruct((ngroups, hr, 2 * D_FF), F32)],
        scratch_shapes=scratch,
        compiler_params=_cparams(("arbitrary", "arbitrary")), name="ffn",
    )(ox, x1, hist, w["x_wo"], w["norm_ffn_g"], w["ffn_w_up"], w["ffn_conv_w"], w["ffn_conv_b"],
      w["ffn_w_down"])


def _rope_tables(pos):
    half = ROPE_DIM // 2
    n = pos.shape[0]
    inv = ROPE_THETA ** (-jnp.arange(half, dtype=F32) * 2.0 / ROPE_DIM)
    ang = pos.astype(F32)[:, None] * inv[None, :]
    cos, sin = jnp.cos(ang), jnp.sin(ang)
    rest = SWA_HEAD_DIM - ROPE_DIM
    z8 = jnp.zeros((n, half), F32)
    zr = jnp.zeros((n, rest), F32)
    rc = jnp.concatenate([cos, cos, jnp.ones((n, rest), F32)], axis=1)
    rs1 = jnp.concatenate([-sin, z8, zr], axis=1)
    rs2 = jnp.concatenate([z8, sin, zr], axis=1)
    rep = LANES // SWA_HEAD_DIM
    return tuple(jnp.tile(a, (1, rep)) for a in (rc, rs1, rs2))


def _block_diag(w):
    n, bi, bj = w.shape
    eye = jnp.eye(n, dtype=w.dtype)
    return (w[:, :, None, :] * eye[:, None, :, None]).reshape(n * bi, n * bj)


def _layer_weights(p, l):
    w_in = p["w_in"][l]
    split = 3344
    w_in_p = jnp.concatenate(
        [w_in[:, :split], jnp.zeros((D_MODEL, AC_PAD), F32), w_in[:, split:]], axis=1).astype(BF16)
    wa2 = jnp.concatenate([p["gla_wa2"][l], jnp.zeros((AC_PAD, GLA_K), F32)], axis=0).astype(BF16)
    row = lambda a: a[l][None, :]
    return dict(
        norm_mix_g=row(p["norm_mix_g"]), w_in=w_in_p, e_swa=_block_ones(SWA_HEADS, SWA_HEAD_DIM),
        swa_qn_g=jnp.tile(p["swa_qn_g"][l], LANES // SWA_HEAD_DIM)[None, :],
        swa_kn_g=jnp.tile(p["swa_kn_g"][l], LANES // SWA_HEAD_DIM)[None, :],
        gla_wa2=wa2, gla_ba=row(p["gla_ba"]),
        lru_conv_w=p["lru_conv_w"][l], lru_conv_b=row(p["lru_conv_b"]),
        lru_wa=_block_diag(p["lru_wa"][l]).astype(BF16), lru_ba=row(p["lru_ba"]),
        lru_wx=_block_diag(p["lru_wx"][l]).astype(BF16), lru_bx=row(p["lru_bx"]),
        lru_lambda=row(p["lru_lambda"]), gla_on_g=row(p["gla_on_g"]),
        w_branch_a=p["w_branch_a"][l].astype(BF16), w_branch_b=p["w_branch_b"][l].astype(BF16),
        w_branch_c=p["w_branch_c"][l].astype(BF16), w_out=p["w_out"][l].astype(BF16),
        norm_x_g=row(p["norm_x_g"]), norm_mem_g=row(p["norm_mem_g"]),
        x_wq=p["x_wq"][l].astype(BF16), x_wk=p["x_wk"][l].astype(BF16),
        x_wv=p["x_wv"][l].astype(BF16), x_qn_g=row(p["x_qn_g"]), x_kn_g=row(p["x_kn_g"]),
        x_wo=p["x_wo"][l].astype(BF16), norm_ffn_g=row(p["norm_ffn_g"]),
        ffn_w_up=p["ffn_w_up"][l].astype(BF16), ffn_conv_w=p["ffn_conv_w"][l],
        ffn_conv_b=row(p["ffn_conv_b"]), ffn_w_down=p["ffn_w_down"][l].astype(BF16),
        swa_sink=p["swa_sink"][l],
    )


def _prompt_layer(x, mem, w, tabs, b, t):
    tm = min(512, t)
    n_mem = mem.shape[0] // b
    (q4, k, v, xb, gy, gq, gk, gv, sr, la, gates) = _in_proj(x, w, tabs, tm)
    oa4 = _swa_prompt(q4, k, v, w["swa_sink"], b, t)
    ob, h_last, conv8 = _lru(xb, gy, jnp.zeros((b, 8, LRU_WIDTH), F32),
                             jnp.zeros((b, 1, LRU_WIDTH), F32), w, 1, min(256, t), b)
    conv_l = conv8[:, 8 - (LRU_CONV_W - 1):]
    oc, st = _gla_prompt(gq, gk, gv, sr, la, w["gla_on_g"], b, t)
    x1, qx = _merge(oa4, ob, oc, gates, x, w, tm)
    mk, mv = _memkv(mem, w, min(512, mem.shape[0]))
    ox = _xattn_prompt(qx, mk, mv, b, t)
    x3, fconv8 = _ffn(ox, x1, jnp.zeros((b, 8, 2 * D_FF), F32), w, 1, tm, b)
    fconv = fconv8[:, 8 - (FFN_CONV_W - 1):]

    k3 = k.reshape(b, t, SWA_KV_HEADS, SWA_HEAD_DIM)[:, -SWA_WINDOW:]
    v3 = v.reshape(b, t, SWA_KV_HEADS, SWA_HEAD_DIM)[:, -SWA_WINDOW:]
    st5 = st.reshape(b, GLA_HEADS // 2, 2, GLA_DV, 2, GLA_DK)
    s_new = jnp.stack([st5[:, :, 0, :, 0, :], st5[:, :, 1, :, 1, :]], axis=2)
    s_new = s_new.reshape(b, GLA_HEADS, GLA_DV, GLA_DK).transpose(0, 1, 3, 2)
    states = (k3, v3, h_last.reshape(b, LRU_WIDTH), conv_l, s_new,
              mk.reshape(b, n_mem, MEM_HEADS, MEM_HEAD_DIM),
              mv.reshape(b, n_mem, MEM_HEADS, MEM_HEAD_DIM), fconv)
    return x3, states


def _tm_to_bm(a, bs, ts):
    return a.reshape(ts, bs, a.shape[-1]).transpose(1, 0, 2)


def _pad8(a):
    return jnp.concatenate([a, jnp.zeros((a.shape[0], 8 - a.shape[1], a.shape[2]), a.dtype)], axis=1)


def _sample_layer(x, w, tabs, st_in, bs, ts, layer):
    (ck, cv, lru_h, lru_conv, gla_s, mem_k, mem_v, ffn_conv) = st_in
    m = bs * ts
    tm = m
    (q4, k, v, xb, gy, gq, gk, gv, sr, la, gates) = _in_proj(x, w, tabs, tm)

    q8 = q4.reshape(4, ts, bs, 2, SWA_HEAD_DIM).transpose(2, 0, 3, 1, 4)
    q8 = q8.reshape(bs, SWA_HEADS, ts, SWA_HEAD_DIM)
    zq = jnp.zeros_like(q8[:, :4])
    qm = jnp.concatenate([jnp.concatenate([q8[:, :4], zq], axis=-1),
                          jnp.concatenate([zq, q8[:, 4:]], axis=-1)], axis=1)
    qm = qm.reshape(bs, SWA_HEADS * ts, LANES)
    k_new = _tm_to_bm(k, bs, ts)
    v_new = _tm_to_bm(v, bs, ts)
    ckf = ck.reshape(bs, SWA_WINDOW, SWA_KV)
    cvf = cv.reshape(bs, SWA_WINDOW, SWA_KV)
    sink_rows = jnp.repeat(w["swa_sink"], ts)[:, None]
    o32 = _swa_step(qm, ckf, _pad8(k_new), cvf, _pad8(v_new), sink_rows, ts)
    o8 = o32.reshape(bs, SWA_HEADS, ts, 2, SWA_HEAD_DIM)
    osel = jnp.concatenate([o8[:, :4, :, 0], o8[:, 4:, :, 1]], axis=1)
    oa4 = osel.reshape(bs, 4, 2, ts, SWA_HEAD_DIM).transpose(1, 3, 0, 2, 4)
    oa4 = oa4.reshape(4, m, LANES).astype(BF16)
    new_k = jnp.concatenate([ckf, k_new], axis=1)[:, -SWA_WINDOW:]
    new_v = jnp.concatenate([cvf, v_new], axis=1)[:, -SWA_WINDOW:]

    hist = lru_conv.transpose(1, 0, 2).reshape(1, (LRU_CONV_W - 1) * bs, LRU_WIDTH)
    ob, h_last, conv_l = _lru(xb, gy, hist, lru_h[None], w, bs, m, 1)
    new_conv = conv_l.reshape(LRU_CONV_W - 1, bs, LRU_WIDTH).transpose(1, 0, 2)

    bm8 = lambda a: _pad8(_tm_to_bm(a, bs, ts))
    s0p = gla_s.reshape(bs, GLA_HEADS // 2, 2 * GLA_DK, GLA_DV)
    oc8, sn = _gla_step(bm8(gq), bm8(gk), bm8(gv), bm8(sr), bm8(la), s0p, w["gla_on_g"], ts)
    oc = oc8[:, :ts].transpose(1, 0, 2).reshape(m, GLA_V)
    new_s = sn.reshape(bs, GLA_HEADS, GLA_DK, GLA_DV)

    x1, qx = _merge(oa4, ob, oc, gates, x, w, tm)
    qb = qx.reshape(ts, bs, MEM_HEADS, MEM_HEAD_DIM).transpose(1, 2, 0, 3)
    eye = jnp.eye(MEM_HEADS, dtype=qb.dtype)
    qexp = (qb[:, :, :, None, :] * eye[None, :, None, :, None]).reshape(bs, MEM_HEADS * ts, MEM_W)
    o5 = _xattn_step(qexp, mem_k, mem_v, layer).reshape(bs, MEM_HEADS, ts, MEM_HEADS, MEM_HEAD_DIM)
    osel = jnp.stack([o5[:, h, :, h, :] for h in range(MEM_HEADS)], axis=2)
    ox = osel.transpose(1, 0, 2, 3).reshape(m, MEM_W).astype(BF16)
    fh = ffn_conv.transpose(1, 0, 2).reshape(1, (FFN_CONV_W - 1) * bs, 2 * D_FF)
    x3, fconv = _ffn(ox, x1, fh, w, bs, m, 1)
    new_fconv = fconv.reshape(FFN_CONV_W - 1, bs, 2 * D_FF).transpose(1, 0, 2)

    states = (new_k.reshape(bs, SWA_WINDOW, SWA_KV_HEADS, SWA_HEAD_DIM),
              new_v.reshape(bs, SWA_WINDOW, SWA_KV_HEADS, SWA_HEAD_DIM),
              h_last.reshape(bs, LRU_WIDTH), new_conv, new_s, new_fconv)
    return x3, states


def kernel(x_prompt, x_sample, cache_swa_k, cache_swa_v, state_lru_h, state_lru_conv, state_gla_s, cache_mem_k, cache_mem_v, state_ffn_conv, mem_prompt, norm_mix_g, w_in, swa_qn_g, swa_kn_g, swa_sink, lru_conv_w, lru_conv_b, lru_wa, lru_ba, lru_wx, lru_bx, lru_lambda, gla_wa2, gla_ba, gla_on_g, w_branch_a, w_branch_b, w_branch_c, w_out, norm_x_g, norm_mem_g, x_wq, x_wk, x_wv, x_qn_g, x_kn_g, x_wo, norm_ffn_g, ffn_w_up, ffn_conv_w, ffn_conv_b, ffn_w_down):
    p = dict(norm_mix_g=norm_mix_g, w_in=w_in, swa_qn_g=swa_qn_g, swa_kn_g=swa_kn_g, swa_sink=swa_sink,
             lru_conv_w=lru_conv_w, lru_conv_b=lru_conv_b, lru_wa=lru_wa, lru_ba=lru_ba, lru_wx=lru_wx,
             lru_bx=lru_bx, lru_lambda=lru_lambda, gla_wa2=gla_wa2, gla_ba=gla_ba, gla_on_g=gla_on_g,
             w_branch_a=w_branch_a, w_branch_b=w_branch_b, w_branch_c=w_branch_c, w_out=w_out,
             norm_x_g=norm_x_g, norm_mem_g=norm_mem_g, x_wq=x_wq, x_wk=x_wk, x_wv=x_wv, x_qn_g=x_qn_g,
             x_kn_g=x_kn_g, x_wo=x_wo, norm_ffn_g=norm_ffn_g, ffn_w_up=ffn_w_up, ffn_conv_w=ffn_conv_w,
             ffn_conv_b=ffn_conv_b, ffn_w_down=ffn_w_down)
    depth = w_in.shape[0]
    bp, tp = x_prompt.shape[:2]
    bs, ts = x_sample.shape[:2]
    tabs_p = _rope_tables(jnp.arange(tp, dtype=jnp.int32))
    tabs_s = _rope_tables(jnp.repeat(PAST_LEN + jnp.arange(ts, dtype=jnp.int32), bs))

    xp = x_prompt.reshape(bp * tp, D_MODEL)
    xs = x_sample.transpose(1, 0, 2).reshape(ts * bs, D_MODEL)
    mem = mem_prompt.reshape(-1, D_MODEL)
    mem_k_all = cache_mem_k.reshape(depth, bs, -1, MEM_W)
    mem_v_all = cache_mem_v.reshape(depth, bs, -1, MEM_W)
    sp, ss = [], []
    for l in range(depth):
        w = _layer_weights(p, l)
        xp, st = _prompt_layer(xp, mem, w, tabs_p, bp, tp)
        sp.append(st)
        xs, st = _sample_layer(xs, w, tabs_s,
                               (cache_swa_k[l], cache_swa_v[l], state_lru_h[l], state_lru_conv[l],
                                state_gla_s[l], mem_k_all, mem_v_all, state_ffn_conv[l]),
                               bs, ts, l)
        ss.append(st)
    y_p = xp.reshape(bp, tp, D_MODEL)
    y_s = xs.reshape(ts, bs, D_MODEL).transpose(1, 0, 2)
    stack = lambda lst, i: jnp.stack([s[i] for s in lst])
    return (y_p, y_s,
            stack(sp, 0), stack(sp, 1), stack(sp, 2), stack(sp, 3), stack(sp, 4), stack(sp, 5),
            stack(sp, 6), stack(sp, 7),
            stack(ss, 0), stack(ss, 1), stack(ss, 2), stack(ss, 3), stack(ss, 4), stack(ss, 5))
```

```python
import functools
import math

import numpy as np
import jax
import jax.numpy as jnp
from jax import lax
from jax.experimental import pallas as pl
from jax.experimental.pallas import tpu as pltpu

F32 = jnp.float32
BF16 = jnp.bfloat16

D_MODEL = 1024
PAST_LEN = 16384
SWA_HEADS = 8
SWA_KV_HEADS = 2
SWA_HEAD_DIM = 64
SWA_WINDOW = 128
ROPE_DIM = 16
ROPE_THETA = 500000.0
LRU_WIDTH = 512
LRU_BLOCKS = 8
LRU_CONV_W = 4
LRU_C = 8.0
GLA_HEADS = 4
GLA_DK = 64
GLA_DV = 128
GLA_RANK = 16
GLA_TAU = 16.0
GLA_CHUNK = 32
MEM_HEADS = 4
MEM_HEAD_DIM = 128
D_FF = 2816
FFN_CONV_W = 3
EPS = 1e-6
NEG = -1e30

SWA_Q = SWA_HEADS * SWA_HEAD_DIM
SWA_KV = SWA_KV_HEADS * SWA_HEAD_DIM
GLA_K = GLA_HEADS * GLA_DK
GLA_V = GLA_HEADS * GLA_DV
MEM_W = MEM_HEADS * MEM_HEAD_DIM

LANES = 128
VMEM_LIMIT = 56 * 1024 * 1024

C_Q = 0
C_KV = 512
C_XB = 768
C_YB = 1280
C_GQK = 1792
C_GV = 2304
C_GR = 2816
C_AC = 3328
C_GATE = 3456
IN_COLS_PAD = 6528
AC_PAD = LANES - GLA_RANK


def _cparams(sem, vmem=VMEM_LIMIT):
    return pltpu.CompilerParams(dimension_semantics=sem, vmem_limit_bytes=vmem)


def _full(shape):
    n = len(shape)
    return pl.BlockSpec(shape, lambda *_: (0,) * n)


def _rows(tm, c):
    return pl.BlockSpec((tm, c), lambda i: (i, 0))


def _rms(x, g):
    ms = jnp.mean(x * x, axis=-1, keepdims=True)
    return x * lax.rsqrt(ms + EPS) * g


def _sigmoid(x):
    return 1.0 / (1.0 + jnp.exp(-x))


def _silu(x):
    return x * _sigmoid(x)


def _gelu_tanh(x):
    return 0.5 * x * (1.0 + jnp.tanh(math.sqrt(2.0 / math.pi) * (x + 0.044715 * (x * x * x))))


def _softplus(z):
    return jnp.maximum(z, 0.0) + jnp.log1p(jnp.exp(-jnp.abs(z)))


def _neg_expm1(x):
    p = x * (1.0 / 120.0) + (1.0 / 24.0)
    for coef in (1.0 / 6.0, 0.5, 1.0):
        p = p * x + coef
    return jnp.where(jnp.abs(x) < 0.125, -(x * p), 1.0 - jnp.exp(x))


def _block_ones(n, blk_r, blk_c=None):
    blk_c = blk_r if blk_c is None else blk_c
    m = np.kron(np.eye(n, dtype=np.float32), np.ones((blk_r, blk_c), np.float32))
    return jnp.asarray(m, dtype=BF16)


def _shift_rows(x, d, row):
    del row
    return pltpu.roll(x, d, 0)


def _rope(x, rc, rs1, rs2):
    return x * rc + pltpu.roll(x, LANES - 8, 1) * rs1 + pltpu.roll(x, 8, 1) * rs2


def _in_proj_kernel(x_ref, g_ref, w_ref, e_ref, qg_ref, kg_ref, rc_ref, rs1_ref, rs2_ref,
                    wa2_ref, ba_ref,
                    q_ref, k_ref, v_ref, xb_ref, gy_ref, gq_ref, gk_ref, gv_ref, sr_ref, la_ref,
                    gate_ref):
    h = _rms(x_ref[...], g_ref[...]).astype(BF16)

    def mm(c0, c1):
        return jnp.dot(h, w_ref[:, c0:c1], preferred_element_type=F32)

    rc, rs1, rs2 = rc_ref[...], rs1_ref[...], rs2_ref[...]
    inv_hd = 1.0 / SWA_HEAD_DIM

    qf = mm(C_Q, C_Q + SWA_Q)
    ss = jnp.dot((qf * qf).astype(BF16), e_ref[...], preferred_element_type=F32)
    qn = qf * lax.rsqrt(ss * inv_hd + EPS)
    for p in range(SWA_Q // LANES):
        qp = qn[:, p * LANES:(p + 1) * LANES] * qg_ref[...]
        q_ref[p] = (_rope(qp, rc, rs1, rs2) * (SWA_HEAD_DIM ** -0.5)).astype(BF16)

    kv = mm(C_KV, C_KV + 2 * SWA_KV)
    kf = kv[:, :SWA_KV]
    ssk = jnp.dot((kf * kf).astype(BF16), e_ref[0:SWA_KV, 0:SWA_KV], preferred_element_type=F32)
    kn = kf * lax.rsqrt(ssk * inv_hd + EPS) * kg_ref[...]
    k_ref[...] = _rope(kn, rc, rs1, rs2)
    v_ref[...] = kv[:, SWA_KV:]

    xb_ref[...] = mm(C_XB, C_XB + LRU_WIDTH).astype(BF16)
    gy_ref[...] = _gelu_tanh(mm(C_YB, C_YB + LRU_WIDTH)).astype(BF16)

    qk = mm(C_GQK, C_GQK + 2 * GLA_K)
    gq_ref[...] = qk[:, :GLA_K] * (GLA_DK ** -0.5)
    gk_ref[...] = qk[:, GLA_K:]
    gv_ref[...] = mm(C_GV, C_GV + GLA_V).astype(BF16)
    sr_ref[...] = _silu(mm(C_GR, C_GR + GLA_V)).astype(BF16)

    ac = mm(C_AC, C_AC + LANES).astype(BF16)
    z = jnp.dot(ac, wa2_ref[...], preferred_element_type=F32) + ba_ref[...]
    la_ref[...] = -_softplus(-z) * (1.0 / GLA_TAU)

    for c in range(3 * D_MODEL // 512):
        gate_ref[:, c * 512:(c + 1) * 512] = _sigmoid(
            mm(C_GATE + c * 512, C_GATE + (c + 1) * 512)).astype(BF16)


def _in_proj(x, w, rope_tabs, tm):
    m = x.shape[0]
    rc, rs1, rs2 = rope_tabs
    ntab = rc.shape[0] // tm
    tab_spec = pl.BlockSpec((tm, LANES), lambda i: (i % ntab, 0))
    outs = [
        jax.ShapeDtypeStruct((SWA_Q // LANES, m, LANES), BF16),
        jax.ShapeDtypeStruct((m, SWA_KV), F32),
        jax.ShapeDtypeStruct((m, SWA_KV), F32),
        jax.ShapeDtypeStruct((m, LRU_WIDTH), BF16),
        jax.ShapeDtypeStruct((m, LRU_WIDTH), BF16),
        jax.ShapeDtypeStruct((m, GLA_K), F32),
        jax.ShapeDtypeStruct((m, GLA_K), F32),
        jax.ShapeDtypeStruct((m, GLA_V), BF16),
        jax.ShapeDtypeStruct((m, GLA_V), BF16),
        jax.ShapeDtypeStruct((m, GLA_K), F32),
        jax.ShapeDtypeStruct((m, 3 * D_MODEL), BF16),
    ]
    out_specs = [
        pl.BlockSpec((SWA_Q // LANES, tm, LANES), lambda i: (0, i, 0)),
        _rows(tm, SWA_KV), _rows(tm, SWA_KV), _rows(tm, LRU_WIDTH), _rows(tm, LRU_WIDTH),
        _rows(tm, GLA_K), _rows(tm, GLA_K), _rows(tm, GLA_V), _rows(tm, GLA_V), _rows(tm, GLA_K),
        _rows(tm, 3 * D_MODEL),
    ]
    in_specs = [
        _rows(tm, D_MODEL), _full((1, D_MODEL)), _full((D_MODEL, IN_COLS_PAD)),
        _full((SWA_Q, SWA_Q)), _full((1, LANES)), _full((1, LANES)),
        tab_spec, tab_spec, tab_spec,
        _full((LANES, GLA_K)), _full((1, GLA_K)),
    ]
    return pl.pallas_call(
        _in_proj_kernel, grid=(m // tm,), in_specs=in_specs, out_specs=out_specs, out_shape=outs,
        compiler_params=_cparams(("parallel",)), name="in_proj",
    )(x, w["norm_mix_g"], w["w_in"], w["e_swa"], w["swa_qn_g"], w["swa_kn_g"], rc, rs1, rs2,
      w["gla_wa2"], w["gla_ba"])


def _sink_softmax(s, sk):
    m = jnp.maximum(jnp.max(s, axis=-1, keepdims=True), sk)
    e = jnp.exp(s - m)
    den = jnp.sum(e, axis=-1, keepdims=True) + jnp.exp(sk - m)
    return e / den


def _swa_prompt_kernel(q_ref, kc_ref, kp_ref, vc_ref, vp_ref, sink_ref, o_ref, *, nblk):
    j = pl.program_id(1)
    w = SWA_WINDOW
    lane = lax.broadcasted_iota(jnp.int32, (1, LANES), 1)
    lo = lane < SWA_HEAD_DIM

    def dup(x, kvh):
        r = pltpu.roll(x, SWA_HEAD_DIM, 1)
        y = jnp.where(lo, x, r) if kvh == 0 else jnp.where(lo, r, x)
        return y.astype(BF16)

    kall = jnp.concatenate([kp_ref[...], kc_ref[...]], axis=0)
    vall = jnp.concatenate([vp_ref[...], vc_ref[...]], axis=0)
    kd = [dup(kall, 0), dup(kall, 1)]
    vd = [dup(vall, 0), dup(vall, 1)]

    qi = lax.broadcasted_iota(jnp.int32, (w, 2 * w), 0) + w
    ki = lax.broadcasted_iota(jnp.int32, (w, 2 * w), 1)
    rel = qi - ki
    band = (rel >= 0) & (rel <= w)
    first = band & ((ki >= w) | ((jnp.zeros_like(ki) + j) > 0))
    row = lax.broadcasted_iota(jnp.int32, (2 * w, 1), 0)

    for c in range(nblk):
        ok = first if c == 0 else band
        ok2 = jnp.concatenate([ok, ok], axis=0)
        for kvh in range(SWA_KV_HEADS):
            kk = kd[kvh][c * w:(c + 2) * w]
            vv = vd[kvh][c * w:(c + 2) * w]
            qg = q_ref[2 * kvh:2 * kvh + 2, c * w:(c + 1) * w, :].reshape(2 * w, LANES)
            res = []
            for par in range(2):
                qm = jnp.where(lo if par == 0 else jnp.logical_not(lo), qg, jnp.zeros_like(qg))
                s = lax.dot_general(qm, kk, (((1,), (1,)), ((), ())), preferred_element_type=F32)
                s = jnp.where(ok2, s, NEG)
                h0 = kvh * 4 + par
                sk = jnp.where(row < w, sink_ref[h0], sink_ref[h0 + 2])
                p = _sink_softmax(s, sk).astype(BF16)
                res.append(jnp.dot(p, vv, preferred_element_type=F32))
            oo = jnp.where(lo, res[0], res[1]).astype(BF16)
            o_ref[2 * kvh:2 * kvh + 2, c * w:(c + 1) * w, :] = oo.reshape(2, w, LANES)


def _swa_prompt(q4, k, v, sink, b, t):
    m = b * t
    w = SWA_WINDOW
    tq = min(512, t)
    nj = t // tq
    nblk = tq // w
    r = t // w
    cur = lambda bb, j: (bb * nj + j, 0)
    prev = lambda bb, j: (jnp.maximum(bb * r + j * nblk - 1, 0), 0)
    qspec = pl.BlockSpec((4, tq, LANES), lambda bb, j: (0, bb * nj + j, 0))
    return pl.pallas_call(
        functools.partial(_swa_prompt_kernel, nblk=nblk),
        grid=(b, nj),
        in_specs=[qspec, pl.BlockSpec((tq, LANES), cur), pl.BlockSpec((w, LANES), prev),
                  pl.BlockSpec((tq, LANES), cur), pl.BlockSpec((w, LANES), prev),
                  pl.BlockSpec(memory_space=pltpu.SMEM)],
        out_specs=qspec,
        out_shape=jax.ShapeDtypeStruct((4, m, LANES), BF16),
        compiler_params=_cparams(("parallel", "parallel")), name="swa_prompt",
    )(q4, k, k, v, v, sink)


def _swa_step_kernel(q_ref, kc_ref, kn_ref, vc_ref, vn_ref, sk_ref, o_ref, *, bb, ts):
    w = SWA_WINDOW
    nk = w + 8
    row_t = lax.broadcasted_iota(jnp.int32, (SWA_HEADS * ts, nk), 0) % ts
    key = lax.broadcasted_iota(jnp.int32, (SWA_HEADS * ts, nk), 1)
    ok = ((key < w) & (key >= row_t)) | ((key >= w) & ((key - w) <= row_t))
    sk = sk_ref[...]
    for i in range(bb):
        kk = jnp.concatenate([kc_ref[i], kn_ref[i]], axis=0).astype(BF16)
        vv = jnp.concatenate([vc_ref[i], vn_ref[i]], axis=0).astype(BF16)
        s = lax.dot_general(q_ref[i], kk, (((1,), (1,)), ((), ())), preferred_element_type=F32)
        s = jnp.where(ok, s, NEG)
        p = _sink_softmax(s, sk).astype(BF16)
        o_ref[i] = jnp.dot(p, vv, preferred_element_type=F32)


def _swa_step(qm, kc, kn8, vc, vn8, sink_rows, ts):
    b = qm.shape[0]
    bb = 8
    nq = SWA_HEADS * ts
    w = SWA_WINDOW
    blk = lambda r: pl.BlockSpec((bb, r, LANES), lambda i: (i, 0, 0))
    return pl.pallas_call(
        functools.partial(_swa_step_kernel, bb=bb, ts=ts),
        grid=(b // bb,),
        in_specs=[blk(nq), blk(w), blk(8), blk(w), blk(8), _full((nq, 1))],
        out_specs=blk(nq),
        out_shape=jax.ShapeDtypeStruct((b, nq, LANES), F32),
        compiler_params=_cparams(("parallel",)), name="swa_step",
    )(qm, kc, kn8, vc, vn8, sink_rows)


def _hist_rows(s, kw):
    return 8 if s == 1 else (kw - 1) * s


def _conv_taps(x, hist, cw_rows, bias, *, s, tm, kw):
    y = bias + x * cw_rows[kw - 1:kw]
    row8 = lax.broadcasted_iota(jnp.int32, (8, 1), 0)
    for k in range(1, kw):
        d = k * s
        if s == 1:
            sh = pltpu.roll(x, d, 0)
            top = jnp.where(row8 < d, pltpu.roll(hist, d, 0), sh[:8])
            sh = jnp.concatenate([top, sh[8:]], axis=0)
        else:
            sh = jnp.concatenate([hist[hist.shape[0] - d:], x[:tm - d]], axis=0)
        y = y + sh * cw_rows[kw - 1 - k:kw - k]
    return y, x[tm - _hist_rows(s, kw):]


def _combine_shifted(a, b, d, keep, axis):
    a_sh = jnp.where(keep, pltpu.roll(a, d, axis), 1.0)
    b_sh = jnp.where(keep, pltpu.roll(b, d, axis), 0.0)
    return a * a_sh, a * b_sh + b


def _scan_adjacent_rows(a, b, hprev, a_sc, b_sc, hin_sc, tm):
    c = a.shape[1]
    ng = tm // 8
    nl = c // LANES
    a3 = a.reshape(ng, 8, c)
    b3 = b.reshape(ng, 8, c)
    r8 = lax.broadcasted_iota(jnp.int32, (1, 8, 1), 1)
    d = 1
    while d < 8:
        a3, b3 = _combine_shifted(a3, b3, d, r8 >= d, 1)
        d *= 2
    a2 = a3.reshape(tm, c)
    b2 = b3.reshape(tm, c)
    for l in range(nl):
        a_sc[l] = a2[:, l * LANES:(l + 1) * LANES]
        b_sc[l] = b2[:, l * LANES:(l + 1) * LANES]
    last = pl.ds(7, ng, stride=8)
    ag = jnp.concatenate([a_sc[l, last, :] for l in range(nl)], axis=1)
    bg = jnp.concatenate([b_sc[l, last, :] for l in range(nl)], axis=1)
    rg = lax.broadcasted_iota(jnp.int32, (ng, 1), 0)
    d = 1
    while d < ng:
        ag, bg = _combine_shifted(ag, bg, d, rg >= d, 0)
        d *= 2
    hg = ag * hprev + bg
    hin_sc[...] = jnp.where(rg >= 1, pltpu.roll(hg, 1, 0), hprev)
    return jnp.concatenate(
        [a2[g * 8:(g + 1) * 8] * hin_sc[g:g + 1, :] + b2[g * 8:(g + 1) * 8] for g in range(ng)],
        axis=0)


def _lru_kernel(xb_ref, gy_ref, hist_ref, h0_ref, cw_ref, cb_ref, wa_ref, ba_ref, wx_ref, bx_ref,
                lam_ref, ob_ref, hl_ref, co_ref, cx_sc, ch_sc, *scan_sc, s, tm):
    j = pl.program_id(1)
    kw = LRU_CONV_W

    @pl.when(j == 0)
    def _():
        cx_sc[...] = hist_ref[...]
        ch_sc[...] = h0_ref[...]

    x = xb_ref[...].astype(F32)
    xc, nh = _conv_taps(x, cx_sc[...], cw_ref[...], cb_ref[...], s=s, tm=tm, kw=kw)
    cx_sc[...] = nh
    co_ref[...] = nh

    xcb = xc.astype(BF16)
    r_g = _sigmoid(jnp.dot(xcb, wa_ref[...], preferred_element_type=F32) + ba_ref[...])
    i_g = _sigmoid(jnp.dot(xcb, wx_ref[...], preferred_element_type=F32) + bx_ref[...])
    decay = (-LRU_C) * _softplus(-lam_ref[...])
    a = jnp.exp(r_g * decay)
    bv = jnp.sqrt(_neg_expm1(r_g * (2.0 * decay))) * (i_g * xc)

    if s == 1:
        h = _scan_adjacent_rows(a, bv, ch_sc[...], *scan_sc, tm)
        hlast = h[tm - 8:, :][7:, :]
    else:
        row = lax.broadcasted_iota(jnp.int32, (tm, 1), 0)
        h = bv + a * jnp.concatenate([ch_sc[...], jnp.zeros((tm - s, LRU_WIDTH), F32)], axis=0)
        d = s
        while d < tm:
            a, h = _combine_shifted(a, h, d, row >= d, 0)
            d *= 2
        hlast = h[tm - s:, :]
    ch_sc[...] = hlast
    hl_ref[...] = hlast
    ob_ref[...] = (gy_ref[...].astype(F32) * h).astype(BF16)


def _lru(xb, gy, hist, h0, w, s, tm, ngroups):
    m = xb.shape[0]
    nj = m // (ngroups * tm)
    kw = LRU_CONV_W
    hr = _hist_rows(s, kw)
    rowspec = pl.BlockSpec((tm, LRU_WIDTH), lambda g, j: (g * nj + j, 0))
    gspec = lambda r: pl.BlockSpec((None, r, LRU_WIDTH), lambda g, j: (g, 0, 0))
    full2 = lambda shape: pl.BlockSpec(shape, lambda g, j: (0, 0))
    scratch = [pltpu.VMEM((hr, LRU_WIDTH), F32), pltpu.VMEM((s, LRU_WIDTH), F32)]
    if s == 1:
        scratch += [pltpu.VMEM((LRU_WIDTH // LANES, tm, LANES), F32),
                    pltpu.VMEM((LRU_WIDTH // LANES, tm, LANES), F32),
                    pltpu.VMEM((tm // 8, LRU_WIDTH), F32)]
    return pl.pallas_call(
        functools.partial(_lru_kernel, s=s, tm=tm),
        grid=(ngroups, nj),
        in_specs=[rowspec, rowspec, gspec(hr), gspec(s),
                  full2((kw, LRU_WIDTH)), full2((1, LRU_WIDTH)),
                  full2((LRU_WIDTH, LRU_WIDTH)), full2((1, LRU_WIDTH)),
                  full2((LRU_WIDTH, LRU_WIDTH)), full2((1, LRU_WIDTH)), full2((1, LRU_WIDTH))],
        out_specs=[rowspec, gspec(s), gspec(hr)],
        out_shape=[jax.ShapeDtypeStruct((m, LRU_WIDTH), BF16),
                   jax.ShapeDtypeStruct((ngroups, s, LRU_WIDTH), F32),
                   jax.ShapeDtypeStruct((ngroups, hr, LRU_WIDTH), F32)],
        scratch_shapes=scratch,
        compiler_params=_cparams(("arbitrary", "arbitrary")), name="lru",
    )(xb, gy, hist, h0, w["lru_conv_w"], w["lru_conv_b"], w["lru_wa"], w["lru_ba"], w["lru_wx"],
      w["lru_bx"], w["lru_lambda"])


def _gla_prompt_kernel(q_ref, k_ref, v_ref, sr_ref, la_ref, g_ref, o_ref, st_ref, s_sc, *, tm):
    j = pl.program_id(1)
    c32 = GLA_CHUNK
    nc = tm // c32

    @pl.when(j == 0)
    def _():
        s_sc[...] = jnp.zeros_like(s_sc)

    row = lax.broadcasted_iota(jnp.int32, (tm, 1), 0)
    rin = row % c32
    cum = la_ref[...]
    d = 1
    while d < c32:
        cum = cum + jnp.where(rin >= d, pltpu.roll(cum, d, 0), 0.0)
        d *= 2
    cum3 = cum.reshape(nc, c32, GLA_K)
    last3 = cum3[:, c32 - 1:c32, :]
    k_all = k_ref[...]
    qt = q_ref[...] * jnp.exp(cum)
    kt = (k_all * jnp.exp(-cum)).astype(BF16)
    kl = (k_all.reshape(nc, c32, GLA_K) * jnp.exp(last3 - cum3)).reshape(tm, GLA_K)
    dec = jnp.exp(last3)

    ti = lax.broadcasted_iota(jnp.int32, (tm, tm), 0)
    si = lax.broadcasted_iota(jnp.int32, (tm, tm), 1)
    causal = (si <= ti) & ((ti // c32) == (si // c32))
    lane = lax.broadcasted_iota(jnp.int32, (1, LANES), 1)
    lo = lane < GLA_DK
    rch = row // c32

    def by_chunk(x):
        return jnp.concatenate([jnp.where(rch == c, x, 0.0).astype(BF16) for c in range(nc)], axis=1)

    for p in range(GLA_HEADS // 2):
        sl = slice(p * LANES, (p + 1) * LANES)
        qtp = qt[:, sl]
        ktp = kt[:, sl]
        kl_x = by_chunk(kl[:, sl])
        for par in range(2):
            h = 2 * p + par
            hs = slice(h * GLA_DV, (h + 1) * GLA_DV)
            qmf = jnp.where(lo if par == 0 else jnp.logical_not(lo), qtp, 0.0)
            qm = qmf.astype(BF16)
            att = lax.dot_general(qm, ktp, (((1,), (1,)), ((), ())), preferred_element_type=F32)
            att = jnp.where(causal, att, 0.0).astype(BF16)
            vh = v_ref[:, hs]
            o_intra = jnp.dot(att, vh, preferred_element_type=F32)
            ut = lax.dot_general(vh, kl_x, (((0,), (0,)), ((), ())), preferred_element_type=F32)
            st = s_sc[h]
            sts = []
            for c in range(nc):
                sts.append(st.astype(BF16))
                st = st * dec[c][:, sl] + ut[:, c * LANES:(c + 1) * LANES]
            s_sc[h] = st
            o_inter = lax.dot_general(by_chunk(qmf), jnp.concatenate(sts, axis=1),
                                      (((1,), (1,)), ((), ())), preferred_element_type=F32)
            o = o_intra + o_inter
            on = _rms(o, g_ref[...])
            o_ref[:, hs] = (on * sr_ref[:, hs].astype(F32)).astype(BF16)
    st_ref[...] = s_sc[...]


def _gla_prompt(gq, gk, gv, sr, la, on_g, b, t):
    m = b * t
    tm = min(256, t)
    nj = t // tm
    rs = lambda c: pl.BlockSpec((tm, c), lambda bb, j: (bb * nj + j, 0))
    sspec = pl.BlockSpec((None, GLA_HEADS, GLA_DV, LANES), lambda bb, j: (bb, 0, 0, 0))
    return pl.pallas_call(
        functools.partial(_gla_prompt_kernel, tm=tm),
        grid=(b, nj),
        in_specs=[rs(GLA_K), rs(GLA_K), rs(GLA_V), rs(GLA_V), rs(GLA_K),
                  pl.BlockSpec((1, GLA_DV), lambda bb, j: (0, 0))],
        out_specs=[rs(GLA_V), sspec],
        out_shape=[jax.ShapeDtypeStruct((m, GLA_V), BF16),
                   jax.ShapeDtypeStruct((b, GLA_HEADS, GLA_DV, LANES), F32)],
        scratch_shapes=[pltpu.VMEM((GLA_HEADS, GLA_DV, LANES), F32)],
        compiler_params=_cparams(("arbitrary", "arbitrary")), name="gla_prompt",
    )(gq, gk, gv, sr, la, on_g)


def _gla_step_kernel(q_ref, k_ref, v_ref, sr_ref, la_ref, s0_ref, g_ref, o_ref, sn_ref, *, bb, ts):
    row = lax.broadcasted_iota(jnp.int32, (8, 1), 0)
    ti = lax.broadcasted_iota(jnp.int32, (8, 8), 0)
    si = lax.broadcasted_iota(jnp.int32, (8, 8), 1)
    causal = si <= ti
    lane = lax.broadcasted_iota(jnp.int32, (1, LANES), 1)
    lo = lane < GLA_DK
    for i in range(bb):
        cum = la_ref[i]
        d = 1
        while d < ts:
            cum = cum + jnp.where(row >= d, pltpu.roll(cum, d, 0), 0.0)
            d *= 2
        last = cum[ts - 1:ts, :]
        k_all = k_ref[i]
        qt = q_ref[i] * jnp.exp(cum)
        kt = (k_all * jnp.exp(-cum)).astype(BF16)
        kl = (k_all * jnp.exp(last - cum)).astype(BF16)
        dec = jnp.exp(last)
        for p in range(GLA_HEADS // 2):
            sl = slice(p * LANES, (p + 1) * LANES)
            sp = s0_ref[i, p]
            dcol = jnp.broadcast_to(dec[:, sl], (LANES, LANES)).T
            spb = sp.astype(BF16)
            klp = kl[:, sl]
            new_rows = []
            for par in range(2):
                h = 2 * p + par
                hs = slice(h * GLA_DV, (h + 1) * GLA_DV)
                qm = jnp.where(lo if par == 0 else jnp.logical_not(lo), qt[:, sl], 0.0).astype(BF16)
                att = lax.dot_general(qm, kt[:, sl], (((1,), (1,)), ((), ())),
                                      preferred_element_type=F32)
                att = jnp.where(causal, att, 0.0).astype(BF16)
                vh = v_ref[i][:, hs]
                o = jnp.dot(att, vh, preferred_element_type=F32) + jnp.dot(
                    qm, spb, preferred_element_type=F32)
                on = _rms(o, g_ref[...])
                o_ref[i, :, hs] = (on * sr_ref[i][:, hs].astype(F32)).astype(BF16)
                u = lax.dot_general(klp, vh, (((0,), (0,)), ((), ())), preferred_element_type=F32)
                new_rows.append(u[par * GLA_DK:(par + 1) * GLA_DK, :])
            sn_ref[i, p] = sp * dcol + jnp.concatenate(new_rows, axis=0)


def _gla_step(gq8, gk8, gv8, sr8, la8, s0p, on_g, ts):
    b = gq8.shape[0]
    bb = 8
    blk = lambda c: pl.BlockSpec((bb, 8, c), lambda i: (i, 0, 0))
    sspec = pl.BlockSpec((bb, GLA_HEADS // 2, LANES, GLA_DV), lambda i: (i, 0, 0, 0))
    return pl.pallas_call(
        functools.partial(_gla_step_kernel, bb=bb, ts=ts),
        grid=(b // bb,),
        in_specs=[blk(GLA_K), blk(GLA_K), blk(GLA_V), blk(GLA_V), blk(GLA_K), sspec,
                  _full((1, GLA_DV))],
        out_specs=[blk(GLA_V), sspec],
        out_shape=[jax.ShapeDtypeStruct((b, 8, GLA_V), BF16),
                   jax.ShapeDtypeStruct((b, GLA_HEADS // 2, LANES, GLA_DV), F32)],
        compiler_params=_cparams(("parallel",)), name="gla_step",
    )(gq8, gk8, gv8, sr8, la8, s0p, on_g)


def _merge_kernel(oa_ref, ob_ref, oc_ref, gate_ref, x_ref, wa_ref, wb_ref, wc_ref, wo_ref,
                  nx_ref, wq_ref, qn_ref, x1_ref, qx_ref):
    oa = jnp.concatenate([oa_ref[p] for p in range(SWA_Q // LANES)], axis=-1)
    dm = D_MODEL
    merged = (gate_ref[:, 0:dm].astype(F32) * jnp.dot(oa, wa_ref[...], preferred_element_type=F32)
              + gate_ref[:, dm:2 * dm].astype(F32)
              * jnp.dot(ob_ref[...], wb_ref[...], preferred_element_type=F32)
              + gate_ref[:, 2 * dm:3 * dm].astype(F32)
              * jnp.dot(oc_ref[...], wc_ref[...], preferred_element_type=F32))
    x1 = x_ref[...] + jnp.dot(merged.astype(BF16), wo_ref[...], preferred_element_type=F32)
    x1_ref[...] = x1
    hx = _rms(x1, nx_ref[...]).astype(BF16)
    q = jnp.dot(hx, wq_ref[...], preferred_element_type=F32)
    for h in range(MEM_HEADS):
        hs = slice(h * MEM_HEAD_DIM, (h + 1) * MEM_HEAD_DIM)
        qx_ref[:, hs] = _rms(q[:, hs], qn_ref[...]).astype(BF16)


def _merge(oa4, ob, oc, gates, x, w, tm):
    m = x.shape[0]
    return pl.pallas_call(
        _merge_kernel, grid=(m // tm,),
        in_specs=[pl.BlockSpec((SWA_Q // LANES, tm, LANES), lambda i: (0, i, 0)),
                  _rows(tm, LRU_WIDTH), _rows(tm, GLA_V), _rows(tm, 3 * D_MODEL), _rows(tm, D_MODEL),
                  _full((SWA_Q, D_MODEL)), _full((LRU_WIDTH, D_MODEL)), _full((GLA_V, D_MODEL)),
                  _full((D_MODEL, D_MODEL)), _full((1, D_MODEL)), _full((D_MODEL, MEM_W)),
                  _full((1, MEM_HEAD_DIM))],
        out_specs=[_rows(tm, D_MODEL), _rows(tm, MEM_W)],
        out_shape=[jax.ShapeDtypeStruct((m, D_MODEL), F32), jax.ShapeDtypeStruct((m, MEM_W), BF16)],
        compiler_params=_cparams(("parallel",)), name="merge",
    )(oa4, ob, oc, gates, x, w["w_branch_a"], w["w_branch_b"], w["w_branch_c"], w["w_out"],
      w["norm_x_g"], w["x_wq"], w["x_qn_g"])


def _memkv_kernel(m_ref, g_ref, wk_ref, wv_ref, kn_ref, k_ref, v_ref):
    hm = _rms(m_ref[...], g_ref[...]).astype(BF16)
    k = jnp.dot(hm, wk_ref[...], preferred_element_type=F32)
    for h in range(MEM_HEADS):
        hs = slice(h * MEM_HEAD_DIM, (h + 1) * MEM_HEAD_DIM)
        k_ref[:, hs] = _rms(k[:, hs], kn_ref[...])
    v_ref[...] = jnp.dot(hm, wv_ref[...], preferred_element_type=F32)


def _memkv(mem, w, tm):
    m = mem.shape[0]
    return pl.pallas_call(
        _memkv_kernel, grid=(m // tm,),
        in_specs=[_rows(tm, D_MODEL), _full((1, D_MODEL)), _full((D_MODEL, MEM_W)),
                  _full((D_MODEL, MEM_W)), _full((1, MEM_HEAD_DIM))],
        out_specs=[_rows(tm, MEM_W), _rows(tm, MEM_W)],
        out_shape=[jax.ShapeDtypeStruct((m, MEM_W), F32), jax.ShapeDtypeStruct((m, MEM_W), F32)],
        compiler_params=_cparams(("parallel",)), name="memkv",
    )(mem, w["norm_mem_g"], w["x_wk"], w["x_wv"], w["x_kn_g"])


def _xattn_heads(q, mk, mv):
    outs = []
    for h in range(MEM_HEADS):
        hs = slice(h * MEM_HEAD_DIM, (h + 1) * MEM_HEAD_DIM)
        s = lax.dot_general(q[:, hs], mk[:, hs].astype(BF16), (((1,), (1,)), ((), ())),
                            preferred_element_type=F32) * (MEM_HEAD_DIM ** -0.5)
        m = jnp.max(s, axis=-1, keepdims=True)
        e = jnp.exp(s - m)
        p = (e / jnp.sum(e, axis=-1, keepdims=True)).astype(BF16)
        outs.append(jnp.dot(p, mv[:, hs].astype(BF16), preferred_element_type=F32))
    return jnp.concatenate(outs, axis=-1)


def _xattn_prompt_kernel(q_ref, mk_ref, mv_ref, o_ref):
    o_ref[...] = _xattn_heads(q_ref[...], mk_ref[...], mv_ref[...]).astype(BF16)


def _xattn_prompt(qx, mk, mv, b, t):
    m = b * t
    n_mem = mk.shape[0] // b
    tq = min(512, t)
    nj = t // tq
    qspec = pl.BlockSpec((tq, MEM_W), lambda bb, j: (bb * nj + j, 0))
    mspec = pl.BlockSpec((n_mem, MEM_W), lambda bb, j: (bb, 0))
    return pl.pallas_call(
        _xattn_prompt_kernel, grid=(b, nj), in_specs=[qspec, mspec, mspec], out_specs=qspec,
        out_shape=jax.ShapeDtypeStruct((m, MEM_W), BF16),
        compiler_params=_cparams(("parallel", "parallel")), name="xattn_prompt",
    )(qx, mk, mv)


def _xattn_step_kernel(q_ref, mk_ref, mv_ref, o_ref, *, bb):
    def heads_on_lanes(ref, i):
        return jnp.concatenate([ref[i, :, h, :] for h in range(MEM_HEADS)], axis=1).astype(BF16)

    for i in range(bb):
        s = lax.dot_general(q_ref[i], heads_on_lanes(mk_ref, i), (((1,), (1,)), ((), ())),
                            preferred_element_type=F32) * (MEM_HEAD_DIM ** -0.5)
        m = jnp.max(s, axis=-1, keepdims=True)
        e = jnp.exp(s - m)
        p = (e / jnp.sum(e, axis=-1, keepdims=True)).astype(BF16)
        o_ref[i] = jnp.dot(p, heads_on_lanes(mv_ref, i), preferred_element_type=F32)


def _xattn_step(qexp, mk_all, mv_all, layer):
    b, nq = qexp.shape[0], qexp.shape[1]
    n_mem = mk_all.shape[2]
    bb = 8
    qspec = pl.BlockSpec((bb, nq, MEM_W), lambda i: (i, 0, 0))
    mspec = pl.BlockSpec((None, bb, n_mem, MEM_HEADS, MEM_HEAD_DIM), lambda i: (layer, i, 0, 0, 0))
    return pl.pallas_call(
        functools.partial(_xattn_step_kernel, bb=bb), grid=(b // bb,),
        in_specs=[qspec, mspec, mspec], out_specs=qspec,
        out_shape=jax.ShapeDtypeStruct((b, nq, MEM_W), F32),
        compiler_params=_cparams(("parallel",)), name="xattn_step",
    )(qexp, mk_all, mv_all)


def _ffn_kernel(ox_ref, x_ref, hist_ref, wo_ref, g_ref, wu_ref, cw_ref, cb_ref, wd_ref,
                x3_ref, co_ref, act_sc, *hs, s, tm):
    kw = FFN_CONV_W
    cwid = 256
    if hs:
        hs_sc = hs[0]

        @pl.when(pl.program_id(1) == 0)
        def _():
            hs_sc[...] = hist_ref[...]
    else:
        hs_sc = hist_ref

    x2 = x_ref[...] + jnp.dot(ox_ref[...], wo_ref[...], preferred_element_type=F32)
    x3_ref[...] = x2
    hf = _rms(x2, g_ref[...]).astype(BF16)
    for c in range(D_FF // cwid):
        ys = []
        for c0 in (c * cwid, D_FF + c * cwid):
            cs = slice(c0, c0 + cwid)
            u = jnp.dot(hf, wu_ref[:, cs], preferred_element_type=F32)
            y, nh = _conv_taps(u, hs_sc[:, cs], cw_ref[:, cs], cb_ref[:, cs], s=s, tm=tm, kw=kw)
            if hs:
                hs_sc[:, cs] = nh
            co_ref[:, cs] = nh
            ys.append(y)
        act_sc[:, c * cwid:(c + 1) * cwid] = (_silu(ys[0]) * ys[1]).astype(BF16)
    x3_ref[...] = x3_ref[...] + jnp.dot(act_sc[...], wd_ref[...], preferred_element_type=F32)


def _ffn(ox, x1, hist, w, s, tm, ngroups):
    m = x1.shape[0]
    nj = m // (ngroups * tm)
    hr = _hist_rows(s, FFN_CONV_W)
    rs = lambda c: pl.BlockSpec((tm, c), lambda g, j: (g * nj + j, 0))
    hspec = pl.BlockSpec((None, hr, 2 * D_FF), lambda g, j: (g, 0, 0), pipeline_mode=pl.Buffered(1))
    ospec = pl.BlockSpec((None, hr, 2 * D_FF), lambda g, j: (g, 0, 0))
    wfull = lambda shape: pl.BlockSpec(shape, lambda g, j: (0, 0), pipeline_mode=pl.Buffered(1))
    scratch = [pltpu.VMEM((tm, D_FF), BF16)]
    if nj > 1:
        scratch.append(pltpu.VMEM((hr, 2 * D_FF), F32))
    return pl.pallas_call(
        functools.partial(_ffn_kernel, s=s, tm=tm),
        grid=(ngroups, nj),
        in_specs=[rs(MEM_W), rs(D_MODEL), hspec, wfull((MEM_W, D_MODEL)), wfull((1, D_MODEL)),
                  wfull((D_MODEL, 2 * D_FF)), wfull((FFN_CONV_W, 2 * D_FF)), wfull((1, 2 * D_FF)),
                  wfull((D_FF, D_MODEL))],
        out_specs=[rs(D_MODEL), ospec],
        out_shape=[jax.ShapeDtypeStruct((m, D_MODEL), F32),
                   jax.ShapeDtypeStruct((ngroups, hr, 2 * D_FF), F32)],
        scratch_shapes=scratch,
        compiler_params=_cparams(("arbitrary", "arbitrary")), name="ffn",
    )(ox, x1, hist, w["x_wo"], w["norm_ffn_g"], w["ffn_w_up"], w["ffn_conv_w"], w["ffn_conv_b"],
      w["ffn_w_down"])


def _rope_tables(pos):
    half = ROPE_DIM // 2
    n = pos.shape[0]
    inv = ROPE_THETA ** (-jnp.arange(half, dtype=F32) * 2.0 / ROPE_DIM)
    ang = pos.astype(F32)[:, None] * inv[None, :]
    cos, sin = jnp.cos(ang), jnp.sin(ang)
    rest = SWA_HEAD_DIM - ROPE_DIM
    z8 = jnp.zeros((n, half), F32)
    zr = jnp.zeros((n, rest), F32)
    rc = jnp.concatenate([cos, cos, jnp.ones((n, rest), F32)], axis=1)
    rs1 = jnp.concatenate([-sin, z8, zr], axis=1)
    rs2 = jnp.concatenate([z8, sin, zr], axis=1)
    rep = LANES // SWA_HEAD_DIM
    return tuple(jnp.tile(a, (1, rep)) for a in (rc, rs1, rs2))


def _block_diag(w):
    n, bi, bj = w.shape
    eye = jnp.eye(n, dtype=w.dtype)
    return (w[:, :, None, :] * eye[:, None, :, None]).reshape(n * bi, n * bj)


def _layer_weights(p, l):
    w_in = p["w_in"][l]
    split = 3344
    w_in_p = jnp.concatenate(
        [w_in[:, :split], jnp.zeros((D_MODEL, AC_PAD), F32), w_in[:, split:]], axis=1).astype(BF16)
    wa2 = jnp.concatenate([p["gla_wa2"][l], jnp.zeros((AC_PAD, GLA_K), F32)], axis=0).astype(BF16)
    row = lambda a: a[l][None, :]
    return dict(
        norm_mix_g=row(p["norm_mix_g"]), w_in=w_in_p, e_swa=_block_ones(SWA_HEADS, SWA_HEAD_DIM),
        swa_qn_g=jnp.tile(p["swa_qn_g"][l], LANES // SWA_HEAD_DIM)[None, :],
        swa_kn_g=jnp.tile(p["swa_kn_g"][l], LANES // SWA_HEAD_DIM)[None, :],
        gla_wa2=wa2, gla_ba=row(p["gla_ba"]),
        lru_conv_w=p["lru_conv_w"][l], lru_conv_b=row(p["lru_conv_b"]),
        lru_wa=_block_diag(p["lru_wa"][l]).astype(BF16), lru_ba=row(p["lru_ba"]),
        lru_wx=_block_diag(p["lru_wx"][l]).astype(BF16), lru_bx=row(p["lru_bx"]),
        lru_lambda=row(p["lru_lambda"]), gla_on_g=row(p["gla_on_g"]),
        w_branch_a=p["w_branch_a"][l].astype(BF16), w_branch_b=p["w_branch_b"][l].astype(BF16),
        w_branch_c=p["w_branch_c"][l].astype(BF16), w_out=p["w_out"][l].astype(BF16),
        norm_x_g=row(p["norm_x_g"]), norm_mem_g=row(p["norm_mem_g"]),
        x_wq=p["x_wq"][l].astype(BF16), x_wk=p["x_wk"][l].astype(BF16),
        x_wv=p["x_wv"][l].astype(BF16), x_qn_g=row(p["x_qn_g"]), x_kn_g=row(p["x_kn_g"]),
        x_wo=p["x_wo"][l].astype(BF16), norm_ffn_g=row(p["norm_ffn_g"]),
        ffn_w_up=p["ffn_w_up"][l].astype(BF16), ffn_conv_w=p["ffn_conv_w"][l],
        ffn_conv_b=row(p["ffn_conv_b"]), ffn_w_down=p["ffn_w_down"][l].astype(BF16),
        swa_sink=p["swa_sink"][l],
    )


def _prompt_layer(x, mem, w, tabs, b, t):
    tm = min(512, t)
    n_mem = mem.shape[0] // b
    (q4, k, v, xb, gy, gq, gk, gv, sr, la, gates) = _in_proj(x, w, tabs, tm)
    oa4 = _swa_prompt(q4, k, v, w["swa_sink"], b, t)
    ob, h_last, conv8 = _lru(xb, gy, jnp.zeros((b, 8, LRU_WIDTH), F32),
                             jnp.zeros((b, 1, LRU_WIDTH), F32), w, 1, min(256, t), b)
    conv_l = conv8[:, 8 - (LRU_CONV_W - 1):]
    oc, st = _gla_prompt(gq, gk, gv, sr, la, w["gla_on_g"], b, t)
    x1, qx = _merge(oa4, ob, oc, gates, x, w, tm)
    mk, mv = _memkv(mem, w, min(512, mem.shape[0]))
    ox = _xattn_prompt(qx, mk, mv, b, t)
    x3, fconv8 = _ffn(ox, x1, jnp.zeros((b, 8, 2 * D_FF), F32), w, 1, tm, b)
    fconv = fconv8[:, 8 - (FFN_CONV_W - 1):]

    k3 = k.reshape(b, t, SWA_KV_HEADS, SWA_HEAD_DIM)[:, -SWA_WINDOW:]
    v3 = v.reshape(b, t, SWA_KV_HEADS, SWA_HEAD_DIM)[:, -SWA_WINDOW:]
    st5 = st.reshape(b, GLA_HEADS // 2, 2, GLA_DV, 2, GLA_DK)
    s_new = jnp.stack([st5[:, :, 0, :, 0, :], st5[:, :, 1, :, 1, :]], axis=2)
    s_new = s_new.reshape(b, GLA_HEADS, GLA_DV, GLA_DK).transpose(0, 1, 3, 2)
    states = (k3, v3, h_last.reshape(b, LRU_WIDTH), conv_l, s_new,
              mk.reshape(b, n_mem, MEM_HEADS, MEM_HEAD_DIM),
              mv.reshape(b, n_mem, MEM_HEADS, MEM_HEAD_DIM), fconv)
    return x3, states


def _tm_to_bm(a, bs, ts):
    return a.reshape(ts, bs, a.shape[-1]).transpose(1, 0, 2)


def _pad8(a):
    return jnp.concatenate([a, jnp.zeros((a.shape[0], 8 - a.shape[1], a.shape[2]), a.dtype)], axis=1)


def _sample_layer(x, w, tabs, st_in, bs, ts, layer):
    (ck, cv, lru_h, lru_conv, gla_s, mem_k, mem_v, ffn_conv) = st_in
    m = bs * ts
    tm = m
    (q4, k, v, xb, gy, gq, gk, gv, sr, la, gates) = _in_proj(x, w, tabs, tm)

    q8 = q4.reshape(4, ts, bs, 2, SWA_HEAD_DIM).transpose(2, 0, 3, 1, 4)
    q8 = q8.reshape(bs, SWA_HEADS, ts, SWA_HEAD_DIM)
    zq = jnp.zeros_like(q8[:, :4])
    qm = jnp.concatenate([jnp.concatenate([q8[:, :4], zq], axis=-1),
                          jnp.concatenate([zq, q8[:, 4:]], axis=-1)], axis=1)
    qm = qm.reshape(bs, SWA_HEADS * ts, LANES)
    k_new = _tm_to_bm(k, bs, ts)
    v_new = _tm_to_bm(v, bs, ts)
    ckf = ck.reshape(bs, SWA_WINDOW, SWA_KV)
    cvf = cv.reshape(bs, SWA_WINDOW, SWA_KV)
    sink_rows = jnp.repeat(w["swa_sink"], ts)[:, None]
    o32 = _swa_step(qm, ckf, _pad8(k_new), cvf, _pad8(v_new), sink_rows, ts)
    o8 = o32.reshape(bs, SWA_HEADS, ts, 2, SWA_HEAD_DIM)
    osel = jnp.concatenate([o8[:, :4, :, 0], o8[:, 4:, :, 1]], axis=1)
    oa4 = osel.reshape(bs, 4, 2, ts, SWA_HEAD_DIM).transpose(1, 3, 0, 2, 4)
    oa4 = oa4.reshape(4, m, LANES).astype(BF16)
    new_k = jnp.concatenate([ckf, k_new], axis=1)[:, -SWA_WINDOW:]
    new_v = jnp.concatenate([cvf, v_new], axis=1)[:, -SWA_WINDOW:]

    hist = lru_conv.transpose(1, 0, 2).reshape(1, (LRU_CONV_W - 1) * bs, LRU_WIDTH)
    ob, h_last, conv_l = _lru(xb, gy, hist, lru_h[None], w, bs, m, 1)
    new_conv = conv_l.reshape(LRU_CONV_W - 1, bs, LRU_WIDTH).transpose(1, 0, 2)

    bm8 = lambda a: _pad8(_tm_to_bm(a, bs, ts))
    s0p = gla_s.reshape(bs, GLA_HEADS // 2, 2 * GLA_DK, GLA_DV)
    oc8, sn = _gla_step(bm8(gq), bm8(gk), bm8(gv), bm8(sr), bm8(la), s0p, w["gla_on_g"], ts)
    oc = oc8[:, :ts].transpose(1, 0, 2).reshape(m, GLA_V)
    new_s = sn.reshape(bs, GLA_HEADS, GLA_DK, GLA_DV)

    x1, qx = _merge(oa4, ob, oc, gates, x, w, tm)
    qb = qx.reshape(ts, bs, MEM_HEADS, MEM_HEAD_DIM).transpose(1, 2, 0, 3)
    eye = jnp.eye(MEM_HEADS, dtype=qb.dtype)
    qexp = (qb[:, :, :, None, :] * eye[None, :, None, :, None]).reshape(bs, MEM_HEADS * ts, MEM_W)
    o5 = _xattn_step(qexp, mem_k, mem_v, layer).reshape(bs, MEM_HEADS, ts, MEM_HEADS, MEM_HEAD_DIM)
    osel = jnp.stack([o5[:, h, :, h, :] for h in range(MEM_HEADS)], axis=2)
    ox = osel.transpose(1, 0, 2, 3).reshape(m, MEM_W).astype(BF16)
    fh = ffn_conv.transpose(1, 0, 2).reshape(1, (FFN_CONV_W - 1) * bs, 2 * D_FF)
    x3, fconv = _ffn(ox, x1, fh, w, bs, m, 1)
    new_fconv = fconv.reshape(FFN_CONV_W - 1, bs, 2 * D_FF).transpose(1, 0, 2)

    states = (new_k.reshape(bs, SWA_WINDOW, SWA_KV_HEADS, SWA_HEAD_DIM),
              new_v.reshape(bs, SWA_WINDOW, SWA_KV_HEADS, SWA_HEAD_DIM),
              h_last.reshape(bs, LRU_WIDTH), new_conv, new_s, new_fconv)
    return x3, states


def kernel(x_prompt, x_sample, cache_swa_k, cache_swa_v, state_lru_h, state_lru_conv, state_gla_s, cache_mem_k, cache_mem_v, state_ffn_conv, mem_prompt, norm_mix_g, w_in, swa_qn_g, swa_kn_g, swa_sink, lru_conv_w, lru_conv_b, lru_wa, lru_ba, lru_wx, lru_bx, lru_lambda, gla_wa2, gla_ba, gla_on_g, w_branch_a, w_branch_b, w_branch_c, w_out, norm_x_g, norm_mem_g, x_wq, x_wk, x_wv, x_qn_g, x_kn_g, x_wo, norm_ffn_g, ffn_w_up, ffn_conv_w, ffn_conv_b, ffn_w_down):
    p = dict(norm_mix_g=norm_mix_g, w_in=w_in, swa_qn_g=swa_qn_g, swa_kn_g=swa_kn_g, swa_sink=swa_sink,
             lru_conv_w=lru_conv_w, lru_conv_b=lru_conv_b, lru_wa=lru_wa, lru_ba=lru_ba, lru_wx=lru_wx,
             lru_bx=lru_bx, lru_lambda=lru_lambda, gla_wa2=gla_wa2, gla_ba=gla_ba, gla_on_g=gla_on_g,
             w_branch_a=w_branch_a, w_branch_b=w_branch_b, w_branch_c=w_branch_c, w_out=w_out,
             norm_x_g=norm_x_g, norm_mem_g=norm_mem_g, x_wq=x_wq, x_wk=x_wk, x_wv=x_wv, x_qn_g=x_qn_g,
             x_kn_g=x_kn_g, x_wo=x_wo, norm_ffn_g=norm_ffn_g, ffn_w_up=ffn_w_up, ffn_conv_w=ffn_conv_w,
             ffn_conv_b=ffn_conv_b, ffn_w_down=ffn_w_down)
    depth = w_in.shape[0]
    bp, tp = x_prompt.shape[:2]
    bs, ts = x_sample.shape[:2]
    tabs_p = _rope_tables(jnp.arange(tp, dtype=jnp.int32))
    tabs_s = _rope_tables(jnp.repeat(PAST_LEN + jnp.arange(ts, dtype=jnp.int32), bs))

    xp = x_prompt.reshape(bp * tp, D_MODEL)
    xs = x_sample.transpose(1, 0, 2).reshape(ts * bs, D_MODEL)
    mem = mem_prompt.reshape(-1, D_MODEL)
    mem_k_all, mem_v_all = cache_mem_k, cache_mem_v
    sp, ss = [], []
    for l in range(depth):
        w = _layer_weights(p, l)
        xp, st = _prompt_layer(xp, mem, w, tabs_p, bp, tp)
        sp.append(st)
        xs, st = _sample_layer(xs, w, tabs_s,
                               (cache_swa_k[l], cache_swa_v[l], state_lru_h[l], state_lru_conv[l],
                                state_gla_s[l], mem_k_all, mem_v_all, state_ffn_conv[l]),
                               bs, ts, l)
        ss.append(st)
    y_p = xp.reshape(bp, tp, D_MODEL)
    y_s = xs.reshape(ts, bs, D_MODEL).transpose(1, 0, 2)
    stack = lambda lst, i: jnp.stack([s[i] for s in lst])
    return (y_p, y_s,
            stack(sp, 0), stack(sp, 1), stack(sp, 2), stack(sp, 3), stack(sp, 4), stack(sp, 5),
            stack(sp, 6), stack(sp, 7),
            stack(ss, 0), stack(ss, 1), stack(ss, 2), stack(ss, 3), stack(ss, 4), stack(ss, 5))
```

```python
import functools
import math

import numpy as np
import jax
import jax.numpy as jnp
from jax import lax
from jax.experimental import pallas as pl
from jax.experimental.pallas import tpu as pltpu

F32 = jnp.float32
BF16 = jnp.bfloat16

D_MODEL = 1024
PAST_LEN = 16384
SWA_HEADS = 8
SWA_KV_HEADS = 2
SWA_HEAD_DIM = 64
SWA_WINDOW = 128
ROPE_DIM = 16
ROPE_THETA = 500000.0
LRU_WIDTH = 512
LRU_BLOCKS = 8
LRU_CONV_W = 4
LRU_C = 8.0
GLA_HEADS = 4
GLA_DK = 64
GLA_DV = 128
GLA_RANK = 16
GLA_TAU = 16.0
GLA_CHUNK = 32
MEM_HEADS = 4
MEM_HEAD_DIM = 128
D_FF = 2816
FFN_CONV_W = 3
EPS = 1e-6
NEG = -1e30

SWA_Q = SWA_HEADS * SWA_HEAD_DIM
SWA_KV = SWA_KV_HEADS * SWA_HEAD_DIM
GLA_K = GLA_HEADS * GLA_DK
GLA_V = GLA_HEADS * GLA_DV
MEM_W = MEM_HEADS * MEM_HEAD_DIM

LANES = 128
VMEM_LIMIT = 56 * 1024 * 1024
TM_PROMPT = 1024
TM_IN_PROJ = 512

C_Q = 0
C_KV = 512
C_XB = 768
C_YB = 1280
C_GQK = 1792
C_GV = 2304
C_GR = 2816
C_AC = 3328
C_GATE = 3456
IN_COLS_PAD = 6528
AC_PAD = LANES - GLA_RANK


def _cparams(sem, vmem=VMEM_LIMIT):
    return pltpu.CompilerParams(dimension_semantics=sem, vmem_limit_bytes=vmem)


def _full(shape):
    n = len(shape)
    return pl.BlockSpec(shape, lambda *_: (0,) * n, pipeline_mode=pl.Buffered(1))


def _rows(tm, c):
    return pl.BlockSpec((tm, c), lambda i: (i, 0))


def _rms(x, g):
    ms = jnp.mean(x * x, axis=-1, keepdims=True)
    return x * lax.rsqrt(ms + EPS) * g


def _sigmoid(x):
    return 1.0 / (1.0 + jnp.exp(-x))


def _silu(x):
    return x * _sigmoid(x)


def _gelu_tanh(x):
    return 0.5 * x * (1.0 + jnp.tanh(math.sqrt(2.0 / math.pi) * (x + 0.044715 * (x * x * x))))


def _softplus(z):
    return jnp.maximum(z, 0.0) + jnp.log1p(jnp.exp(-jnp.abs(z)))


def _neg_expm1(x):
    p = x * (1.0 / 120.0) + (1.0 / 24.0)
    for coef in (1.0 / 6.0, 0.5, 1.0):
        p = p * x + coef
    return jnp.where(jnp.abs(x) < 0.125, -(x * p), 1.0 - jnp.exp(x))


def _block_ones(n, blk_r, blk_c=None):
    blk_c = blk_r if blk_c is None else blk_c
    m = np.kron(np.eye(n, dtype=np.float32), np.ones((blk_r, blk_c), np.float32))
    return jnp.asarray(m, dtype=BF16)


def _shift_rows(x, d, row):
    del row
    return pltpu.roll(x, d, 0)


def _rope(x, rc, rs1, rs2):
    return x * rc + pltpu.roll(x, LANES - 8, 1) * rs1 + pltpu.roll(x, 8, 1) * rs2


def _in_proj_kernel(x_ref, g_ref, w_ref, e_ref, qg_ref, kg_ref, rc_ref, rs1_ref, rs2_ref,
                    wa2_ref, ba_ref,
                    q_ref, k_ref, v_ref, xb_ref, gy_ref, gq_ref, gk_ref, gv_ref, sr_ref, la_ref,
                    gate_ref):
    h = _rms(x_ref[...], g_ref[...]).astype(BF16)

    def mm(c0, c1):
        return jnp.dot(h, w_ref[:, c0:c1], preferred_element_type=F32)

    rc, rs1, rs2 = rc_ref[...], rs1_ref[...], rs2_ref[...]
    inv_hd = 1.0 / SWA_HEAD_DIM

    qf = mm(C_Q, C_Q + SWA_Q)
    ss = jnp.dot((qf * qf).astype(BF16), e_ref[...], preferred_element_type=F32)
    qn = qf * lax.rsqrt(ss * inv_hd + EPS)
    for p in range(SWA_Q // LANES):
        qp = qn[:, p * LANES:(p + 1) * LANES] * qg_ref[...]
        q_ref[p] = (_rope(qp, rc, rs1, rs2) * (SWA_HEAD_DIM ** -0.5)).astype(BF16)

    kv = mm(C_KV, C_KV + 2 * SWA_KV)
    kf = kv[:, :SWA_KV]
    ssk = jnp.dot((kf * kf).astype(BF16), e_ref[0:SWA_KV, 0:SWA_KV], preferred_element_type=F32)
    kn = kf * lax.rsqrt(ssk * inv_hd + EPS) * kg_ref[...]
    k_ref[...] = _rope(kn, rc, rs1, rs2)
    v_ref[...] = kv[:, SWA_KV:]

    xb_ref[...] = mm(C_XB, C_XB + LRU_WIDTH).astype(BF16)
    gy_ref[...] = _gelu_tanh(mm(C_YB, C_YB + LRU_WIDTH)).astype(BF16)

    qk = mm(C_GQK, C_GQK + 2 * GLA_K)
    gq_ref[...] = qk[:, :GLA_K] * (GLA_DK ** -0.5)
    gk_ref[...] = qk[:, GLA_K:]
    gv_ref[...] = mm(C_GV, C_GV + GLA_V).astype(BF16)
    sr_ref[...] = _silu(mm(C_GR, C_GR + GLA_V)).astype(BF16)

    ac = mm(C_AC, C_AC + LANES).astype(BF16)
    z = jnp.dot(ac, wa2_ref[...], preferred_element_type=F32) + ba_ref[...]
    la_ref[...] = -_softplus(-z) * (1.0 / GLA_TAU)

    for c in range(3 * D_MODEL // 512):
        gate_ref[:, c * 512:(c + 1) * 512] = _sigmoid(
            mm(C_GATE + c * 512, C_GATE + (c + 1) * 512)).astype(BF16)


def _in_proj(x, w, rope_tabs, tm):
    m = x.shape[0]
    rc, rs1, rs2 = rope_tabs
    ntab = rc.shape[0] // tm
    tab_spec = pl.BlockSpec((tm, LANES), lambda i: (i % ntab, 0))
    outs = [
        jax.ShapeDtypeStruct((SWA_Q // LANES, m, LANES), BF16),
        jax.ShapeDtypeStruct((m, SWA_KV), F32),
        jax.ShapeDtypeStruct((m, SWA_KV), F32),
        jax.ShapeDtypeStruct((m, LRU_WIDTH), BF16),
        jax.ShapeDtypeStruct((m, LRU_WIDTH), BF16),
        jax.ShapeDtypeStruct((m, GLA_K), F32),
        jax.ShapeDtypeStruct((m, GLA_K), F32),
        jax.ShapeDtypeStruct((m, GLA_V), BF16),
        jax.ShapeDtypeStruct((m, GLA_V), BF16),
        jax.ShapeDtypeStruct((m, GLA_K), F32),
        jax.ShapeDtypeStruct((m, 3 * D_MODEL), BF16),
    ]
    out_specs = [
        pl.BlockSpec((SWA_Q // LANES, tm, LANES), lambda i: (0, i, 0)),
        _rows(tm, SWA_KV), _rows(tm, SWA_KV), _rows(tm, LRU_WIDTH), _rows(tm, LRU_WIDTH),
        _rows(tm, GLA_K), _rows(tm, GLA_K), _rows(tm, GLA_V), _rows(tm, GLA_V), _rows(tm, GLA_K),
        _rows(tm, 3 * D_MODEL),
    ]
    in_specs = [
        _rows(tm, D_MODEL), _full((1, D_MODEL)), _full((D_MODEL, IN_COLS_PAD)),
        _full((SWA_Q, SWA_Q)), _full((1, LANES)), _full((1, LANES)),
        tab_spec, tab_spec, tab_spec,
        _full((LANES, GLA_K)), _full((1, GLA_K)),
    ]
    return pl.pallas_call(
        _in_proj_kernel, grid=(m // tm,), in_specs=in_specs, out_specs=out_specs, out_shape=outs,
        compiler_params=_cparams(("parallel",)), name="in_proj",
    )(x, w["norm_mix_g"], w["w_in"], w["e_swa"], w["swa_qn_g"], w["swa_kn_g"], rc, rs1, rs2,
      w["gla_wa2"], w["gla_ba"])


def _sink_softmax(s, sk):
    m = jnp.maximum(jnp.max(s, axis=-1, keepdims=True), sk)
    e = jnp.exp(s - m)
    den = jnp.sum(e, axis=-1, keepdims=True) + jnp.exp(sk - m)
    return e / den


def _swa_prompt_kernel(q_ref, kc_ref, kp_ref, vc_ref, vp_ref, sink_ref, o_ref, *, nblk):
    j = pl.program_id(1)
    w = SWA_WINDOW
    lane = lax.broadcasted_iota(jnp.int32, (1, LANES), 1)
    lo = lane < SWA_HEAD_DIM

    def dup(x, kvh):
        r = pltpu.roll(x, SWA_HEAD_DIM, 1)
        y = jnp.where(lo, x, r) if kvh == 0 else jnp.where(lo, r, x)
        return y.astype(BF16)

    kall = jnp.concatenate([kp_ref[...], kc_ref[...]], axis=0)
    vall = jnp.concatenate([vp_ref[...], vc_ref[...]], axis=0)
    kd = [dup(kall, 0), dup(kall, 1)]
    vd = [dup(vall, 0), dup(vall, 1)]

    qi = lax.broadcasted_iota(jnp.int32, (w, 2 * w), 0) + w
    ki = lax.broadcasted_iota(jnp.int32, (w, 2 * w), 1)
    rel = qi - ki
    band = (rel >= 0) & (rel <= w)
    first = band & ((ki >= w) | ((jnp.zeros_like(ki) + j) > 0))
    row = lax.broadcasted_iota(jnp.int32, (2 * w, 1), 0)

    for c in range(nblk):
        ok = first if c == 0 else band
        ok2 = jnp.concatenate([ok, ok], axis=0)
        for kvh in range(SWA_KV_HEADS):
            kk = kd[kvh][c * w:(c + 2) * w]
            vv = vd[kvh][c * w:(c + 2) * w]
            qg = q_ref[2 * kvh:2 * kvh + 2, c * w:(c + 1) * w, :].reshape(2 * w, LANES)
            res = []
            for par in range(2):
                qm = jnp.where(lo if par == 0 else jnp.logical_not(lo), qg, jnp.zeros_like(qg))
                s = lax.dot_general(qm, kk, (((1,), (1,)), ((), ())), preferred_element_type=F32)
                s = jnp.where(ok2, s, NEG)
                h0 = kvh * 4 + par
                sk = jnp.where(row < w, sink_ref[h0], sink_ref[h0 + 2])
                p = _sink_softmax(s, sk).astype(BF16)
                res.append(jnp.dot(p, vv, preferred_element_type=F32))
            oo = jnp.where(lo, res[0], res[1]).astype(BF16)
            o_ref[2 * kvh:2 * kvh + 2, c * w:(c + 1) * w, :] = oo.reshape(2, w, LANES)


def _swa_prompt(q4, k, v, sink, b, t):
    m = b * t
    w = SWA_WINDOW
    tq = min(512, t)
    nj = t // tq
    nblk = tq // w
    r = t // w
    cur = lambda bb, j: (bb * nj + j, 0)
    prev = lambda bb, j: (jnp.maximum(bb * r + j * nblk - 1, 0), 0)
    qspec = pl.BlockSpec((4, tq, LANES), lambda bb, j: (0, bb * nj + j, 0))
    return pl.pallas_call(
        functools.partial(_swa_prompt_kernel, nblk=nblk),
        grid=(b, nj),
        in_specs=[qspec, pl.BlockSpec((tq, LANES), cur), pl.BlockSpec((w, LANES), prev),
                  pl.BlockSpec((tq, LANES), cur), pl.BlockSpec((w, LANES), prev),
                  pl.BlockSpec(memory_space=pltpu.SMEM)],
        out_specs=qspec,
        out_shape=jax.ShapeDtypeStruct((4, m, LANES), BF16),
        compiler_params=_cparams(("parallel", "parallel")), name="swa_prompt",
    )(q4, k, k, v, v, sink)


def _swa_step_kernel(q_ref, kc_ref, kn_ref, vc_ref, vn_ref, sk_ref, o_ref, *, bb, ts):
    w = SWA_WINDOW
    nk = w + 8
    row_t = lax.broadcasted_iota(jnp.int32, (SWA_HEADS * ts, nk), 0) % ts
    key = lax.broadcasted_iota(jnp.int32, (SWA_HEADS * ts, nk), 1)
    ok = ((key < w) & (key >= row_t)) | ((key >= w) & ((key - w) <= row_t))
    sk = sk_ref[...]
    for i in range(bb):
        kk = jnp.concatenate([kc_ref[i], kn_ref[i]], axis=0).astype(BF16)
        vv = jnp.concatenate([vc_ref[i], vn_ref[i]], axis=0).astype(BF16)
        s = lax.dot_general(q_ref[i], kk, (((1,), (1,)), ((), ())), preferred_element_type=F32)
        s = jnp.where(ok, s, NEG)
        p = _sink_softmax(s, sk).astype(BF16)
        o_ref[i] = jnp.dot(p, vv, preferred_element_type=F32)


def _swa_step(qm, kc, kn8, vc, vn8, sink_rows, ts):
    b = qm.shape[0]
    bb = 8
    nq = SWA_HEADS * ts
    w = SWA_WINDOW
    blk = lambda r: pl.BlockSpec((bb, r, LANES), lambda i: (i, 0, 0))
    return pl.pallas_call(
        functools.partial(_swa_step_kernel, bb=bb, ts=ts),
        grid=(b // bb,),
        in_specs=[blk(nq), blk(w), blk(8), blk(w), blk(8), _full((nq, 1))],
        out_specs=blk(nq),
        out_shape=jax.ShapeDtypeStruct((b, nq, LANES), F32),
        compiler_params=_cparams(("parallel",)), name="swa_step",
    )(qm, kc, kn8, vc, vn8, sink_rows)


def _hist_rows(s, kw):
    return 8 if s == 1 else (kw - 1) * s


def _conv_taps(x, hist, cw_rows, bias, *, s, tm, kw):
    y = bias + x * cw_rows[kw - 1:kw]
    row8 = lax.broadcasted_iota(jnp.int32, (8, 1), 0)
    for k in range(1, kw):
        d = k * s
        if s == 1:
            sh = pltpu.roll(x, d, 0)
            top = jnp.where(row8 < d, pltpu.roll(hist, d, 0), sh[:8])
            sh = jnp.concatenate([top, sh[8:]], axis=0)
        else:
            sh = jnp.concatenate([hist[hist.shape[0] - d:], x[:tm - d]], axis=0)
        y = y + sh * cw_rows[kw - 1 - k:kw - k]
    return y, x[tm - _hist_rows(s, kw):]


def _combine_shifted(a, b, d, keep, axis):
    a_sh = jnp.where(keep, pltpu.roll(a, d, axis), 1.0)
    b_sh = jnp.where(keep, pltpu.roll(b, d, axis), 0.0)
    return a * a_sh, a * b_sh + b


def _scan_adjacent_rows(a, b, hprev, a_sc, b_sc, hin_sc, tm):
    c = a.shape[1]
    ng = tm // 8
    nl = c // LANES
    a3 = a.reshape(ng, 8, c)
    b3 = b.reshape(ng, 8, c)
    r8 = lax.broadcasted_iota(jnp.int32, (1, 8, 1), 1)
    d = 1
    while d < 8:
        a3, b3 = _combine_shifted(a3, b3, d, r8 >= d, 1)
        d *= 2
    a2 = a3.reshape(tm, c)
    b2 = b3.reshape(tm, c)
    for l in range(nl):
        a_sc[l] = a2[:, l * LANES:(l + 1) * LANES]
        b_sc[l] = b2[:, l * LANES:(l + 1) * LANES]
    last = pl.ds(7, ng, stride=8)
    ag = jnp.concatenate([a_sc[l, last, :] for l in range(nl)], axis=1)
    bg = jnp.concatenate([b_sc[l, last, :] for l in range(nl)], axis=1)
    rg = lax.broadcasted_iota(jnp.int32, (ng, 1), 0)
    d = 1
    while d < ng:
        ag, bg = _combine_shifted(ag, bg, d, rg >= d, 0)
        d *= 2
    hg = ag * hprev + bg
    hin_sc[...] = jnp.where(rg >= 1, pltpu.roll(hg, 1, 0), hprev)
    return jnp.concatenate(
        [a2[g * 8:(g + 1) * 8] * hin_sc[g:g + 1, :] + b2[g * 8:(g + 1) * 8] for g in range(ng)],
        axis=0)


def _lru_kernel(xb_ref, gy_ref, hist_ref, h0_ref, cw_ref, cb_ref, wa_ref, ba_ref, wx_ref, bx_ref,
                lam_ref, ob_ref, hl_ref, co_ref, cx_sc, ch_sc, *scan_sc, s, tm):
    j = pl.program_id(1)
    kw = LRU_CONV_W

    @pl.when(j == 0)
    def _():
        cx_sc[...] = hist_ref[...]
        ch_sc[...] = h0_ref[...]

    x = xb_ref[...].astype(F32)
    xc, nh = _conv_taps(x, cx_sc[...], cw_ref[...], cb_ref[...], s=s, tm=tm, kw=kw)
    cx_sc[...] = nh
    co_ref[...] = nh

    xcb = xc.astype(BF16)
    r_g = _sigmoid(jnp.dot(xcb, wa_ref[...], preferred_element_type=F32) + ba_ref[...])
    i_g = _sigmoid(jnp.dot(xcb, wx_ref[...], preferred_element_type=F32) + bx_ref[...])
    decay = (-LRU_C) * _softplus(-lam_ref[...])
    a = jnp.exp(r_g * decay)
    bv = jnp.sqrt(_neg_expm1(r_g * (2.0 * decay))) * (i_g * xc)

    if s == 1:
        h = _scan_adjacent_rows(a, bv, ch_sc[...], *scan_sc, tm)
        hlast = h[tm - 8:, :][7:, :]
    else:
        row = lax.broadcasted_iota(jnp.int32, (tm, 1), 0)
        h = bv + a * jnp.concatenate([ch_sc[...], jnp.zeros((tm - s, LRU_WIDTH), F32)], axis=0)
        d = s
        while d < tm:
            a, h = _combine_shifted(a, h, d, row >= d, 0)
            d *= 2
        hlast = h[tm - s:, :]
    ch_sc[...] = hlast
    hl_ref[...] = hlast
    ob_ref[...] = (gy_ref[...].astype(F32) * h).astype(BF16)


def _lru(xb, gy, hist, h0, w, s, tm, ngroups):
    m = xb.shape[0]
    nj = m // (ngroups * tm)
    kw = LRU_CONV_W
    hr = _hist_rows(s, kw)
    rowspec = pl.BlockSpec((tm, LRU_WIDTH), lambda g, j: (g * nj + j, 0))
    gspec = lambda r: pl.BlockSpec((None, r, LRU_WIDTH), lambda g, j: (g, 0, 0))
    full2 = lambda shape: pl.BlockSpec(shape, lambda g, j: (0, 0))
    scratch = [pltpu.VMEM((hr, LRU_WIDTH), F32), pltpu.VMEM((s, LRU_WIDTH), F32)]
    if s == 1:
        scratch += [pltpu.VMEM((LRU_WIDTH // LANES, tm, LANES), F32),
                    pltpu.VMEM((LRU_WIDTH // LANES, tm, LANES), F32),
                    pltpu.VMEM((tm // 8, LRU_WIDTH), F32)]
    return pl.pallas_call(
        functools.partial(_lru_kernel, s=s, tm=tm),
        grid=(ngroups, nj),
        in_specs=[rowspec, rowspec, gspec(hr), gspec(s),
                  full2((kw, LRU_WIDTH)), full2((1, LRU_WIDTH)),
                  full2((LRU_WIDTH, LRU_WIDTH)), full2((1, LRU_WIDTH)),
                  full2((LRU_WIDTH, LRU_WIDTH)), full2((1, LRU_WIDTH)), full2((1, LRU_WIDTH))],
        out_specs=[rowspec, gspec(s), gspec(hr)],
        out_shape=[jax.ShapeDtypeStruct((m, LRU_WIDTH), BF16),
                   jax.ShapeDtypeStruct((ngroups, s, LRU_WIDTH), F32),
                   jax.ShapeDtypeStruct((ngroups, hr, LRU_WIDTH), F32)],
        scratch_shapes=scratch,
        compiler_params=_cparams(("arbitrary", "arbitrary")), name="lru",
    )(xb, gy, hist, h0, w["lru_conv_w"], w["lru_conv_b"], w["lru_wa"], w["lru_ba"], w["lru_wx"],
      w["lru_bx"], w["lru_lambda"])


def _gla_prompt_kernel(q_ref, k_ref, v_ref, sr_ref, la_ref, g_ref, o_ref, st_ref, s_sc, *, tm):
    j = pl.program_id(1)
    c32 = GLA_CHUNK
    nc = tm // c32

    @pl.when(j == 0)
    def _():
        s_sc[...] = jnp.zeros_like(s_sc)

    row = lax.broadcasted_iota(jnp.int32, (tm, 1), 0)
    rin = row % c32
    cum = la_ref[...]
    d = 1
    while d < c32:
        cum = cum + jnp.where(rin >= d, pltpu.roll(cum, d, 0), 0.0)
        d *= 2
    cum3 = cum.reshape(nc, c32, GLA_K)
    last3 = cum3[:, c32 - 1:c32, :]
    k_all = k_ref[...]
    qt = q_ref[...] * jnp.exp(cum)
    kt = (k_all * jnp.exp(-cum)).astype(BF16)
    kl = (k_all.reshape(nc, c32, GLA_K) * jnp.exp(last3 - cum3)).reshape(tm, GLA_K)
    dec = jnp.exp(last3)

    ti = lax.broadcasted_iota(jnp.int32, (tm, tm), 0)
    si = lax.broadcasted_iota(jnp.int32, (tm, tm), 1)
    causal = (si <= ti) & ((ti // c32) == (si // c32))
    lane = lax.broadcasted_iota(jnp.int32, (1, LANES), 1)
    lo = lane < GLA_DK
    rch = row // c32

    def by_chunk(x):
        return jnp.concatenate([jnp.where(rch == c, x, 0.0).astype(BF16) for c in range(nc)], axis=1)

    for p in range(GLA_HEADS // 2):
        sl = slice(p * LANES, (p + 1) * LANES)
        qtp = qt[:, sl]
        ktp = kt[:, sl]
        kl_x = by_chunk(kl[:, sl])
        for par in range(2):
            h = 2 * p + par
            hs = slice(h * GLA_DV, (h + 1) * GLA_DV)
            qmf = jnp.where(lo if par == 0 else jnp.logical_not(lo), qtp, 0.0)
            qm = qmf.astype(BF16)
            att = lax.dot_general(qm, ktp, (((1,), (1,)), ((), ())), preferred_element_type=F32)
            att = jnp.where(causal, att, 0.0).astype(BF16)
            vh = v_ref[:, hs]
            o_intra = jnp.dot(att, vh, preferred_element_type=F32)
            ut = lax.dot_general(vh, kl_x, (((0,), (0,)), ((), ())), preferred_element_type=F32)
            st = s_sc[h]
            sts = []
            for c in range(nc):
                sts.append(st.astype(BF16))
                st = st * dec[c][:, sl] + ut[:, c * LANES:(c + 1) * LANES]
            s_sc[h] = st
            o_inter = lax.dot_general(by_chunk(qmf), jnp.concatenate(sts, axis=1),
                                      (((1,), (1,)), ((), ())), preferred_element_type=F32)
            o = o_intra + o_inter
            on = _rms(o, g_ref[...])
            o_ref[:, hs] = (on * sr_ref[:, hs].astype(F32)).astype(BF16)
    st_ref[...] = s_sc[...]


def _gla_prompt(gq, gk, gv, sr, la, on_g, b, t):
    m = b * t
    tm = min(256, t)
    nj = t // tm
    rs = lambda c: pl.BlockSpec((tm, c), lambda bb, j: (bb * nj + j, 0))
    sspec = pl.BlockSpec((None, GLA_HEADS, GLA_DV, LANES), lambda bb, j: (bb, 0, 0, 0))
    return pl.pallas_call(
        functools.partial(_gla_prompt_kernel, tm=tm),
        grid=(b, nj),
        in_specs=[rs(GLA_K), rs(GLA_K), rs(GLA_V), rs(GLA_V), rs(GLA_K),
                  pl.BlockSpec((1, GLA_DV), lambda bb, j: (0, 0))],
        out_specs=[rs(GLA_V), sspec],
        out_shape=[jax.ShapeDtypeStruct((m, GLA_V), BF16),
                   jax.ShapeDtypeStruct((b, GLA_HEADS, GLA_DV, LANES), F32)],
        scratch_shapes=[pltpu.VMEM((GLA_HEADS, GLA_DV, LANES), F32)],
        compiler_params=_cparams(("arbitrary", "arbitrary")), name="gla_prompt",
    )(gq, gk, gv, sr, la, on_g)


def _gla_step_kernel(q_ref, k_ref, v_ref, sr_ref, la_ref, s0_ref, g_ref, o_ref, sn_ref, *, bb, ts):
    row = lax.broadcasted_iota(jnp.int32, (8, 1), 0)
    ti = lax.broadcasted_iota(jnp.int32, (8, 8), 0)
    si = lax.broadcasted_iota(jnp.int32, (8, 8), 1)
    causal = si <= ti
    lane = lax.broadcasted_iota(jnp.int32, (1, LANES), 1)
    lo = lane < GLA_DK
    for i in range(bb):
        cum = la_ref[i]
        d = 1
        while d < ts:
            cum = cum + jnp.where(row >= d, pltpu.roll(cum, d, 0), 0.0)
            d *= 2
        last = cum[ts - 1:ts, :]
        k_all = k_ref[i]
        qt = q_ref[i] * jnp.exp(cum)
        kt = (k_all * jnp.exp(-cum)).astype(BF16)
        kl = (k_all * jnp.exp(last - cum)).astype(BF16)
        dec = jnp.exp(last)
        for p in range(GLA_HEADS // 2):
            sl = slice(p * LANES, (p + 1) * LANES)
            sp = s0_ref[i, p]
            dcol = jnp.broadcast_to(dec[:, sl], (LANES, LANES)).T
            spb = sp.astype(BF16)
            klp = kl[:, sl]
            new_rows = []
            for par in range(2):
                h = 2 * p + par
                hs = slice(h * GLA_DV, (h + 1) * GLA_DV)
                qm = jnp.where(lo if par == 0 else jnp.logical_not(lo), qt[:, sl], 0.0).astype(BF16)
                att = lax.dot_general(qm, kt[:, sl], (((1,), (1,)), ((), ())),
                                      preferred_element_type=F32)
                att = jnp.where(causal, att, 0.0).astype(BF16)
                vh = v_ref[i][:, hs]
                o = jnp.dot(att, vh, preferred_element_type=F32) + jnp.dot(
                    qm, spb, preferred_element_type=F32)
                on = _rms(o, g_ref[...])
                o_ref[i, :, hs] = (on * sr_ref[i][:, hs].astype(F32)).astype(BF16)
                u = lax.dot_general(klp, vh, (((0,), (0,)), ((), ())), preferred_element_type=F32)
                new_rows.append(u[par * GLA_DK:(par + 1) * GLA_DK, :])
            sn_ref[i, p] = sp * dcol + jnp.concatenate(new_rows, axis=0)


def _gla_step(gq8, gk8, gv8, sr8, la8, s0p, on_g, ts):
    b = gq8.shape[0]
    bb = 8
    blk = lambda c: pl.BlockSpec((bb, 8, c), lambda i: (i, 0, 0))
    sspec = pl.BlockSpec((bb, GLA_HEADS // 2, LANES, GLA_DV), lambda i: (i, 0, 0, 0))
    return pl.pallas_call(
        functools.partial(_gla_step_kernel, bb=bb, ts=ts),
        grid=(b // bb,),
        in_specs=[blk(GLA_K), blk(GLA_K), blk(GLA_V), blk(GLA_V), blk(GLA_K), sspec,
                  _full((1, GLA_DV))],
        out_specs=[blk(GLA_V), sspec],
        out_shape=[jax.ShapeDtypeStruct((b, 8, GLA_V), BF16),
                   jax.ShapeDtypeStruct((b, GLA_HEADS // 2, LANES, GLA_DV), F32)],
        compiler_params=_cparams(("parallel",)), name="gla_step",
    )(gq8, gk8, gv8, sr8, la8, s0p, on_g)


def _merge_kernel(oa_ref, ob_ref, oc_ref, gate_ref, x_ref, wa_ref, wb_ref, wc_ref, wo_ref,
                  nx_ref, wq_ref, qn_ref, x1_ref, qx_ref):
    oa = jnp.concatenate([oa_ref[p] for p in range(SWA_Q // LANES)], axis=-1)
    dm = D_MODEL
    merged = (gate_ref[:, 0:dm].astype(F32) * jnp.dot(oa, wa_ref[...], preferred_element_type=F32)
              + gate_ref[:, dm:2 * dm].astype(F32)
              * jnp.dot(ob_ref[...], wb_ref[...], preferred_element_type=F32)
              + gate_ref[:, 2 * dm:3 * dm].astype(F32)
              * jnp.dot(oc_ref[...], wc_ref[...], preferred_element_type=F32))
    x1 = x_ref[...] + jnp.dot(merged.astype(BF16), wo_ref[...], preferred_element_type=F32)
    x1_ref[...] = x1
    hx = _rms(x1, nx_ref[...]).astype(BF16)
    q = jnp.dot(hx, wq_ref[...], preferred_element_type=F32)
    for h in range(MEM_HEADS):
        hs = slice(h * MEM_HEAD_DIM, (h + 1) * MEM_HEAD_DIM)
        qx_ref[:, hs] = _rms(q[:, hs], qn_ref[...]).astype(BF16)


def _merge(oa4, ob, oc, gates, x, w, tm):
    m = x.shape[0]
    return pl.pallas_call(
        _merge_kernel, grid=(m // tm,),
        in_specs=[pl.BlockSpec((SWA_Q // LANES, tm, LANES), lambda i: (0, i, 0)),
                  _rows(tm, LRU_WIDTH), _rows(tm, GLA_V), _rows(tm, 3 * D_MODEL), _rows(tm, D_MODEL),
                  _full((SWA_Q, D_MODEL)), _full((LRU_WIDTH, D_MODEL)), _full((GLA_V, D_MODEL)),
                  _full((D_MODEL, D_MODEL)), _full((1, D_MODEL)), _full((D_MODEL, MEM_W)),
                  _full((1, MEM_HEAD_DIM))],
        out_specs=[_rows(tm, D_MODEL), _rows(tm, MEM_W)],
        out_shape=[jax.ShapeDtypeStruct((m, D_MODEL), F32), jax.ShapeDtypeStruct((m, MEM_W), BF16)],
        compiler_params=_cparams(("parallel",)), name="merge",
    )(oa4, ob, oc, gates, x, w["w_branch_a"], w["w_branch_b"], w["w_branch_c"], w["w_out"],
      w["norm_x_g"], w["x_wq"], w["x_qn_g"])


def _memkv_kernel(m_ref, g_ref, wk_ref, wv_ref, kn_ref, k_ref, v_ref):
    hm = _rms(m_ref[...], g_ref[...]).astype(BF16)
    k = jnp.dot(hm, wk_ref[...], preferred_element_type=F32)
    for h in range(MEM_HEADS):
        hs = slice(h * MEM_HEAD_DIM, (h + 1) * MEM_HEAD_DIM)
        k_ref[:, hs] = _rms(k[:, hs], kn_ref[...])
    v_ref[...] = jnp.dot(hm, wv_ref[...], preferred_element_type=F32)


def _memkv(mem, w, tm):
    m = mem.shape[0]
    return pl.pallas_call(
        _memkv_kernel, grid=(m // tm,),
        in_specs=[_rows(tm, D_MODEL), _full((1, D_MODEL)), _full((D_MODEL, MEM_W)),
                  _full((D_MODEL, MEM_W)), _full((1, MEM_HEAD_DIM))],
        out_specs=[_rows(tm, MEM_W), _rows(tm, MEM_W)],
        out_shape=[jax.ShapeDtypeStruct((m, MEM_W), F32), jax.ShapeDtypeStruct((m, MEM_W), F32)],
        compiler_params=_cparams(("parallel",)), name="memkv",
    )(mem, w["norm_mem_g"], w["x_wk"], w["x_wv"], w["x_kn_g"])


def _xattn_heads(q, mk, mv):
    outs = []
    for h in range(MEM_HEADS):
        hs = slice(h * MEM_HEAD_DIM, (h + 1) * MEM_HEAD_DIM)
        s = lax.dot_general(q[:, hs], mk[:, hs].astype(BF16), (((1,), (1,)), ((), ())),
                            preferred_element_type=F32) * (MEM_HEAD_DIM ** -0.5)
        m = jnp.max(s, axis=-1, keepdims=True)
        e = jnp.exp(s - m)
        p = (e / jnp.sum(e, axis=-1, keepdims=True)).astype(BF16)
        outs.append(jnp.dot(p, mv[:, hs].astype(BF16), preferred_element_type=F32))
    return jnp.concatenate(outs, axis=-1)


def _xattn_prompt_kernel(q_ref, mk_ref, mv_ref, o_ref):
    o_ref[...] = _xattn_heads(q_ref[...], mk_ref[...], mv_ref[...]).astype(BF16)


def _xattn_prompt(qx, mk, mv, b, t):
    m = b * t
    n_mem = mk.shape[0] // b
    tq = min(512, t)
    nj = t // tq
    qspec = pl.BlockSpec((tq, MEM_W), lambda bb, j: (bb * nj + j, 0))
    mspec = pl.BlockSpec((n_mem, MEM_W), lambda bb, j: (bb, 0))
    return pl.pallas_call(
        _xattn_prompt_kernel, grid=(b, nj), in_specs=[qspec, mspec, mspec], out_specs=qspec,
        out_shape=jax.ShapeDtypeStruct((m, MEM_W), BF16),
        compiler_params=_cparams(("parallel", "parallel")), name="xattn_prompt",
    )(qx, mk, mv)


def _xattn_step_kernel(q_ref, mk_ref, mv_ref, o_ref, *, bb):
    def heads_on_lanes(ref, i):
        return jnp.concatenate([ref[i, :, h, :] for h in range(MEM_HEADS)], axis=1).astype(BF16)

    for i in range(bb):
        s = lax.dot_general(q_ref[i], heads_on_lanes(mk_ref, i), (((1,), (1,)), ((), ())),
                            preferred_element_type=F32) * (MEM_HEAD_DIM ** -0.5)
        m = jnp.max(s, axis=-1, keepdims=True)
        e = jnp.exp(s - m)
        p = (e / jnp.sum(e, axis=-1, keepdims=True)).astype(BF16)
        o_ref[i] = jnp.dot(p, heads_on_lanes(mv_ref, i), preferred_element_type=F32)


def _xattn_step(qexp, mk_all, mv_all, layer):
    b, nq = qexp.shape[0], qexp.shape[1]
    n_mem = mk_all.shape[2]
    bb = 8
    qspec = pl.BlockSpec((bb, nq, MEM_W), lambda i: (i, 0, 0))
    mspec = pl.BlockSpec((None, bb, n_mem, MEM_HEADS, MEM_HEAD_DIM), lambda i: (layer, i, 0, 0, 0))
    return pl.pallas_call(
        functools.partial(_xattn_step_kernel, bb=bb), grid=(b // bb,),
        in_specs=[qspec, mspec, mspec], out_specs=qspec,
        out_shape=jax.ShapeDtypeStruct((b, nq, MEM_W), F32),
        compiler_params=_cparams(("parallel",)), name="xattn_step",
    )(qexp, mk_all, mv_all)


def _ffn_kernel(ox_ref, x_ref, hist_ref, wo_ref, g_ref, wu_ref, cw_ref, cb_ref, wd_ref,
                x3_ref, co_ref, act_sc, *hs, s, tm):
    kw = FFN_CONV_W
    cwid = 256
    if hs:
        hs_sc = hs[0]

        @pl.when(pl.program_id(1) == 0)
        def _():
            hs_sc[...] = hist_ref[...]
    else:
        hs_sc = hist_ref

    x2 = x_ref[...] + jnp.dot(ox_ref[...], wo_ref[...], preferred_element_type=F32)
    x3_ref[...] = x2
    hf = _rms(x2, g_ref[...]).astype(BF16)
    for c in range(D_FF // cwid):
        ys = []
        for c0 in (c * cwid, D_FF + c * cwid):
            cs = slice(c0, c0 + cwid)
            u = jnp.dot(hf, wu_ref[:, cs], preferred_element_type=F32)
            y, nh = _conv_taps(u, hs_sc[:, cs], cw_ref[:, cs], cb_ref[:, cs], s=s, tm=tm, kw=kw)
            if hs:
                hs_sc[:, cs] = nh
            co_ref[:, cs] = nh
            ys.append(y)
        act_sc[:, c * cwid:(c + 1) * cwid] = (_silu(ys[0]) * ys[1]).astype(BF16)
    x3_ref[...] = x3_ref[...] + jnp.dot(act_sc[...], wd_ref[...], preferred_element_type=F32)


def _ffn(ox, x1, hist, w, s, tm, ngroups):
    m = x1.shape[0]
    nj = m // (ngroups * tm)
    hr = _hist_rows(s, FFN_CONV_W)
    rs = lambda c: pl.BlockSpec((tm, c), lambda g, j: (g * nj + j, 0))
    hspec = pl.BlockSpec((None, hr, 2 * D_FF), lambda g, j: (g, 0, 0), pipeline_mode=pl.Buffered(1))
    ospec = pl.BlockSpec((None, hr, 2 * D_FF), lambda g, j: (g, 0, 0))
    wfull = lambda shape: pl.BlockSpec(shape, lambda g, j: (0, 0), pipeline_mode=pl.Buffered(1))
    scratch = [pltpu.VMEM((tm, D_FF), BF16)]
    if nj > 1:
        scratch.append(pltpu.VMEM((hr, 2 * D_FF), F32))
    return pl.pallas_call(
        functools.partial(_ffn_kernel, s=s, tm=tm),
        grid=(ngroups, nj),
        in_specs=[rs(MEM_W), rs(D_MODEL), hspec, wfull((MEM_W, D_MODEL)), wfull((1, D_MODEL)),
                  wfull((D_MODEL, 2 * D_FF)), wfull((FFN_CONV_W, 2 * D_FF)), wfull((1, 2 * D_FF)),
                  wfull((D_FF, D_MODEL))],
        out_specs=[rs(D_MODEL), ospec],
        out_shape=[jax.ShapeDtypeStruct((m, D_MODEL), F32),
                   jax.ShapeDtypeStruct((ngroups, hr, 2 * D_FF), F32)],
        scratch_shapes=scratch,
        compiler_params=_cparams(("arbitrary", "arbitrary")), name="ffn",
    )(ox, x1, hist, w["x_wo"], w["norm_ffn_g"], w["ffn_w_up"], w["ffn_conv_w"], w["ffn_conv_b"],
      w["ffn_w_down"])


def _rope_tables(pos):
    half = ROPE_DIM // 2
    n = pos.shape[0]
    inv = ROPE_THETA ** (-jnp.arange(half, dtype=F32) * 2.0 / ROPE_DIM)
    ang = pos.astype(F32)[:, None] * inv[None, :]
    cos, sin = jnp.cos(ang), jnp.sin(ang)
    rest = SWA_HEAD_DIM - ROPE_DIM
    z8 = jnp.zeros((n, half), F32)
    zr = jnp.zeros((n, rest), F32)
    rc = jnp.concatenate([cos, cos, jnp.ones((n, rest), F32)], axis=1)
    rs1 = jnp.concatenate([-sin, z8, zr], axis=1)
    rs2 = jnp.concatenate([z8, sin, zr], axis=1)
    rep = LANES // SWA_HEAD_DIM
    return tuple(jnp.tile(a, (1, rep)) for a in (rc, rs1, rs2))


def _block_diag(w):
    n, bi, bj = w.shape
    eye = jnp.eye(n, dtype=w.dtype)
    return (w[:, :, None, :] * eye[:, None, :, None]).reshape(n * bi, n * bj)


def _layer_weights(p, l):
    w_in = p["w_in"][l]
    split = 3344
    w_in_p = jnp.concatenate(
        [w_in[:, :split], jnp.zeros((D_MODEL, AC_PAD), F32), w_in[:, split:]], axis=1).astype(BF16)
    wa2 = jnp.concatenate([p["gla_wa2"][l], jnp.zeros((AC_PAD, GLA_K), F32)], axis=0).astype(BF16)
    row = lambda a: a[l][None, :]
    return dict(
        norm_mix_g=row(p["norm_mix_g"]), w_in=w_in_p, e_swa=_block_ones(SWA_HEADS, SWA_HEAD_DIM),
        swa_qn_g=jnp.tile(p["swa_qn_g"][l], LANES // SWA_HEAD_DIM)[None, :],
        swa_kn_g=jnp.tile(p["swa_kn_g"][l], LANES // SWA_HEAD_DIM)[None, :],
        gla_wa2=wa2, gla_ba=row(p["gla_ba"]),
        lru_conv_w=p["lru_conv_w"][l], lru_conv_b=row(p["lru_conv_b"]),
        lru_wa=_block_diag(p["lru_wa"][l]).astype(BF16), lru_ba=row(p["lru_ba"]),
        lru_wx=_block_diag(p["lru_wx"][l]).astype(BF16), lru_bx=row(p["lru_bx"]),
        lru_lambda=row(p["lru_lambda"]), gla_on_g=row(p["gla_on_g"]),
        w_branch_a=p["w_branch_a"][l].astype(BF16), w_branch_b=p["w_branch_b"][l].astype(BF16),
        w_branch_c=p["w_branch_c"][l].astype(BF16), w_out=p["w_out"][l].astype(BF16),
        norm_x_g=row(p["norm_x_g"]), norm_mem_g=row(p["norm_mem_g"]),
        x_wq=p["x_wq"][l].astype(BF16), x_wk=p["x_wk"][l].astype(BF16),
        x_wv=p["x_wv"][l].astype(BF16), x_qn_g=row(p["x_qn_g"]), x_kn_g=row(p["x_kn_g"]),
        x_wo=p["x_wo"][l].astype(BF16), norm_ffn_g=row(p["norm_ffn_g"]),
        ffn_w_up=p["ffn_w_up"][l].astype(BF16), ffn_conv_w=p["ffn_conv_w"][l],
        ffn_conv_b=row(p["ffn_conv_b"]), ffn_w_down=p["ffn_w_down"][l].astype(BF16),
        swa_sink=p["swa_sink"][l],
    )


def _prompt_layer(x, mem, w, tabs, b, t):
    tm = min(TM_PROMPT, t)
    n_mem = mem.shape[0] // b
    (q4, k, v, xb, gy, gq, gk, gv, sr, la, gates) = _in_proj(x, w, tabs, min(TM_IN_PROJ, t))
    oa4 = _swa_prompt(q4, k, v, w["swa_sink"], b, t)
    ob, h_last, conv8 = _lru(xb, gy, jnp.zeros((b, 8, LRU_WIDTH), F32),
                             jnp.zeros((b, 1, LRU_WIDTH), F32), w, 1, min(512, t), b)
    conv_l = conv8[:, 8 - (LRU_CONV_W - 1):]
    oc, st = _gla_prompt(gq, gk, gv, sr, la, w["gla_on_g"], b, t)
    x1, qx = _merge(oa4, ob, oc, gates, x, w, tm)
    mk, mv = _memkv(mem, w, min(512, mem.shape[0]))
    ox = _xattn_prompt(qx, mk, mv, b, t)
    x3, fconv8 = _ffn(ox, x1, jnp.zeros((b, 8, 2 * D_FF), F32), w, 1, tm, b)
    fconv = fconv8[:, 8 - (FFN_CONV_W - 1):]

    k3 = k.reshape(b, t, SWA_KV_HEADS, SWA_HEAD_DIM)[:, -SWA_WINDOW:]
    v3 = v.reshape(b, t, SWA_KV_HEADS, SWA_HEAD_DIM)[:, -SWA_WINDOW:]
    st5 = st.reshape(b, GLA_HEADS // 2, 2, GLA_DV, 2, GLA_DK)
    s_new = jnp.stack([st5[:, :, 0, :, 0, :], st5[:, :, 1, :, 1, :]], axis=2)
    s_new = s_new.reshape(b, GLA_HEADS, GLA_DV, GLA_DK).transpose(0, 1, 3, 2)
    states = (k3, v3, h_last.reshape(b, LRU_WIDTH), conv_l, s_new,
              mk.reshape(b, n_mem, MEM_HEADS, MEM_HEAD_DIM),
              mv.reshape(b, n_mem, MEM_HEADS, MEM_HEAD_DIM), fconv)
    return x3, states


def _tm_to_bm(a, bs, ts):
    return a.reshape(ts, bs, a.shape[-1]).transpose(1, 0, 2)


def _pad8(a):
    return jnp.concatenate([a, jnp.zeros((a.shape[0], 8 - a.shape[1], a.shape[2]), a.dtype)], axis=1)


def _sample_layer(x, w, tabs, st_in, bs, ts, layer):
    (ck, cv, lru_h, lru_conv, gla_s, mem_k, mem_v, ffn_conv) = st_in
    m = bs * ts
    tm = m
    (q4, k, v, xb, gy, gq, gk, gv, sr, la, gates) = _in_proj(x, w, tabs, tm)

    q8 = q4.reshape(4, ts, bs, 2, SWA_HEAD_DIM).transpose(2, 0, 3, 1, 4)
    q8 = q8.reshape(bs, SWA_HEADS, ts, SWA_HEAD_DIM)
    zq = jnp.zeros_like(q8[:, :4])
    qm = jnp.concatenate([jnp.concatenate([q8[:, :4], zq], axis=-1),
                          jnp.concatenate([zq, q8[:, 4:]], axis=-1)], axis=1)
    qm = qm.reshape(bs, SWA_HEADS * ts, LANES)
    k_new = _tm_to_bm(k, bs, ts)
    v_new = _tm_to_bm(v, bs, ts)
    ckf = ck.reshape(bs, SWA_WINDOW, SWA_KV)
    cvf = cv.reshape(bs, SWA_WINDOW, SWA_KV)
    sink_rows = jnp.repeat(w["swa_sink"], ts)[:, None]
    o32 = _swa_step(qm, ckf, _pad8(k_new), cvf, _pad8(v_new), sink_rows, ts)
    o8 = o32.reshape(bs, SWA_HEADS, ts, 2, SWA_HEAD_DIM)
    osel = jnp.concatenate([o8[:, :4, :, 0], o8[:, 4:, :, 1]], axis=1)
    oa4 = osel.reshape(bs, 4, 2, ts, SWA_HEAD_DIM).transpose(1, 3, 0, 2, 4)
    oa4 = oa4.reshape(4, m, LANES).astype(BF16)
    new_k = jnp.concatenate([ckf, k_new], axis=1)[:, -SWA_WINDOW:]
    new_v = jnp.concatenate([cvf, v_new], axis=1)[:, -SWA_WINDOW:]

    hist = lru_conv.transpose(1, 0, 2).reshape(1, (LRU_CONV_W - 1) * bs, LRU_WIDTH)
    ob, h_last, conv_l = _lru(xb, gy, hist, lru_h[None], w, bs, m, 1)
    new_conv = conv_l.reshape(LRU_CONV_W - 1, bs, LRU_WIDTH).transpose(1, 0, 2)

    bm8 = lambda a: _pad8(_tm_to_bm(a, bs, ts))
    s0p = gla_s.reshape(bs, GLA_HEADS // 2, 2 * GLA_DK, GLA_DV)
    oc8, sn = _gla_step(bm8(gq), bm8(gk), bm8(gv), bm8(sr), bm8(la), s0p, w["gla_on_g"], ts)
    oc = oc8[:, :ts].transpose(1, 0, 2).reshape(m, GLA_V)
    new_s = sn.reshape(bs, GLA_HEADS, GLA_DK, GLA_DV)

    x1, qx = _merge(oa4, ob, oc, gates, x, w, tm)
    qb = qx.reshape(ts, bs, MEM_HEADS, MEM_HEAD_DIM).transpose(1, 2, 0, 3)
    eye = jnp.eye(MEM_HEADS, dtype=qb.dtype)
    qexp = (qb[:, :, :, None, :] * eye[None, :, None, :, None]).reshape(bs, MEM_HEADS * ts, MEM_W)
    o5 = _xattn_step(qexp, mem_k, mem_v, layer).reshape(bs, MEM_HEADS, ts, MEM_HEADS, MEM_HEAD_DIM)
    osel = jnp.stack([o5[:, h, :, h, :] for h in range(MEM_HEADS)], axis=2)
    ox = osel.transpose(1, 0, 2, 3).reshape(m, MEM_W).astype(BF16)
    fh = ffn_conv.transpose(1, 0, 2).reshape(1, (FFN_CONV_W - 1) * bs, 2 * D_FF)
    x3, fconv = _ffn(ox, x1, fh, w, bs, m, 1)
    new_fconv = fconv.reshape(FFN_CONV_W - 1, bs, 2 * D_FF).transpose(1, 0, 2)

    states = (new_k.reshape(bs, SWA_WINDOW, SWA_KV_HEADS, SWA_HEAD_DIM),
              new_v.reshape(bs, SWA_WINDOW, SWA_KV_HEADS, SWA_HEAD_DIM),
              h_last.reshape(bs, LRU_WIDTH), new_conv, new_s, new_fconv)
    return x3, states


def kernel(x_prompt, x_sample, cache_swa_k, cache_swa_v, state_lru_h, state_lru_conv, state_gla_s, cache_mem_k, cache_mem_v, state_ffn_conv, mem_prompt, norm_mix_g, w_in, swa_qn_g, swa_kn_g, swa_sink, lru_conv_w, lru_conv_b, lru_wa, lru_ba, lru_wx, lru_bx, lru_lambda, gla_wa2, gla_ba, gla_on_g, w_branch_a, w_branch_b, w_branch_c, w_out, norm_x_g, norm_mem_g, x_wq, x_wk, x_wv, x_qn_g, x_kn_g, x_wo, norm_ffn_g, ffn_w_up, ffn_conv_w, ffn_conv_b, ffn_w_down):
    p = dict(norm_mix_g=norm_mix_g, w_in=w_in, swa_qn_g=swa_qn_g, swa_kn_g=swa_kn_g, swa_sink=swa_sink,
             lru_conv_w=lru_conv_w, lru_conv_b=lru_conv_b, lru_wa=lru_wa, lru_ba=lru_ba, lru_wx=lru_wx,
             lru_bx=lru_bx, lru_lambda=lru_lambda, gla_wa2=gla_wa2, gla_ba=gla_ba, gla_on_g=gla_on_g,
             w_branch_a=w_branch_a, w_branch_b=w_branch_b, w_branch_c=w_branch_c, w_out=w_out,
             norm_x_g=norm_x_g, norm_mem_g=norm_mem_g, x_wq=x_wq, x_wk=x_wk, x_wv=x_wv, x_qn_g=x_qn_g,
             x_kn_g=x_kn_g, x_wo=x_wo, norm_ffn_g=norm_ffn_g, ffn_w_up=ffn_w_up, ffn_conv_w=ffn_conv_w,
             ffn_conv_b=ffn_conv_b, ffn_w_down=ffn_w_down)
    depth = w_in.shape[0]
    bp, tp = x_prompt.shape[:2]
    bs, ts = x_sample.shape[:2]
    tabs_p = _rope_tables(jnp.arange(tp, dtype=jnp.int32))
    tabs_s = _rope_tables(jnp.repeat(PAST_LEN + jnp.arange(ts, dtype=jnp.int32), bs))

    xp = x_prompt.reshape(bp * tp, D_MODEL)
    xs = x_sample.transpose(1, 0, 2).reshape(ts * bs, D_MODEL)
    mem = mem_prompt.reshape(-1, D_MODEL)
    mem_k_all, mem_v_all = cache_mem_k, cache_mem_v
    sp, ss = [], []
    for l in range(depth):
        w = _layer_weights(p, l)
        xp, st = _prompt_layer(xp, mem, w, tabs_p, bp, tp)
        sp.append(st)
        xs, st = _sample_layer(xs, w, tabs_s,
                               (cache_swa_k[l], cache_swa_v[l], state_lru_h[l], state_lru_conv[l],
                                state_gla_s[l], mem_k_all, mem_v_all, state_ffn_conv[l]),
                               bs, ts, l)
        ss.append(st)
    y_p = xp.reshape(bp, tp, D_MODEL)
    y_s = xs.reshape(ts, bs, D_MODEL).transpose(1, 0, 2)
    stack = lambda lst, i: jnp.stack([s[i] for s in lst])
    return (y_p, y_s,
            stack(sp, 0), stack(sp, 1), stack(sp, 2), stack(sp, 3), stack(sp, 4), stack(sp, 5),
            stack(sp, 6), stack(sp, 7),
            stack(ss, 0), stack(ss, 1), stack(ss, 2), stack(ss, 3), stack(ss, 4), stack(ss, 5))
```

```python
import functools
import math

import numpy as np
import jax
import jax.numpy as jnp
from jax import lax
from jax.experimental import pallas as pl
from jax.experimental.pallas import tpu as pltpu

F32 = jnp.float32
BF16 = jnp.bfloat16

D_MODEL = 1024
PAST_LEN = 16384
SWA_HEADS = 8
SWA_KV_HEADS = 2
SWA_HEAD_DIM = 64
SWA_WINDOW = 128
ROPE_DIM = 16
ROPE_THETA = 500000.0
LRU_WIDTH = 512
LRU_BLOCKS = 8
LRU_CONV_W = 4
LRU_C = 8.0
GLA_HEADS = 4
GLA_DK = 64
GLA_DV = 128
GLA_RANK = 16
GLA_TAU = 16.0
GLA_CHUNK = 32
MEM_HEADS = 4
MEM_HEAD_DIM = 128
D_FF = 2816
FFN_CONV_W = 3
EPS = 1e-6
NEG = -1e30

SWA_Q = SWA_HEADS * SWA_HEAD_DIM
SWA_KV = SWA_KV_HEADS * SWA_HEAD_DIM
GLA_K = GLA_HEADS * GLA_DK
GLA_V = GLA_HEADS * GLA_DV
MEM_W = MEM_HEADS * MEM_HEAD_DIM

LANES = 128
VMEM_LIMIT = 56 * 1024 * 1024
TM_PROMPT = 1024
TM_IN_PROJ = 512

C_Q = 0
C_KV = 512
C_XB = 768
C_YB = 1280
C_GQK = 1792
C_GV = 2304
C_GR = 2816
C_AC = 3328
C_GATE = 3456
IN_COLS_PAD = 6528
AC_PAD = LANES - GLA_RANK


def _cparams(sem, vmem=VMEM_LIMIT):
    return pltpu.CompilerParams(dimension_semantics=sem, vmem_limit_bytes=vmem)


def _full(shape):
    n = len(shape)
    return pl.BlockSpec(shape, lambda *_: (0,) * n, pipeline_mode=pl.Buffered(1))


def _rows(tm, c):
    return pl.BlockSpec((tm, c), lambda i: (i, 0))


def _rms(x, g):
    ms = jnp.mean(x * x, axis=-1, keepdims=True)
    return x * lax.rsqrt(ms + EPS) * g


def _sigmoid(x):
    return 1.0 / (1.0 + jnp.exp(-x))


def _silu(x):
    return x * _sigmoid(x)


def _gelu_tanh(x):
    return 0.5 * x * (1.0 + jnp.tanh(math.sqrt(2.0 / math.pi) * (x + 0.044715 * (x * x * x))))


def _softplus(z):
    return jnp.maximum(z, 0.0) + jnp.log1p(jnp.exp(-jnp.abs(z)))


def _neg_expm1(x):
    p = x * (1.0 / 120.0) + (1.0 / 24.0)
    for coef in (1.0 / 6.0, 0.5, 1.0):
        p = p * x + coef
    return jnp.where(jnp.abs(x) < 0.125, -(x * p), 1.0 - jnp.exp(x))


def _block_ones(n, blk_r, blk_c=None):
    blk_c = blk_r if blk_c is None else blk_c
    m = np.kron(np.eye(n, dtype=np.float32), np.ones((blk_r, blk_c), np.float32))
    return jnp.asarray(m, dtype=BF16)


def _shift_rows(x, d, row):
    del row
    return pltpu.roll(x, d, 0)


def _rope(x, rc, rs1, rs2):
    return x * rc + pltpu.roll(x, LANES - 8, 1) * rs1 + pltpu.roll(x, 8, 1) * rs2


def _in_proj_kernel(x_ref, g_ref, w_ref, e_ref, qg_ref, kg_ref, rc_ref, rs1_ref, rs2_ref,
                    wa2_ref, ba_ref,
                    q_ref, k_ref, v_ref, xb_ref, gy_ref, gq_ref, gk_ref, gv_ref, sr_ref, la_ref,
                    gate_ref):
    h = _rms(x_ref[...], g_ref[...]).astype(BF16)

    def mm(c0, c1):
        return jnp.dot(h, w_ref[:, c0:c1], preferred_element_type=F32)

    rc, rs1, rs2 = rc_ref[...], rs1_ref[...], rs2_ref[...]
    inv_hd = 1.0 / SWA_HEAD_DIM

    qf = mm(C_Q, C_Q + SWA_Q)
    ss = jnp.dot((qf * qf).astype(BF16), e_ref[...], preferred_element_type=F32)
    qn = qf * lax.rsqrt(ss * inv_hd + EPS)
    for p in range(SWA_Q // LANES):
        qp = qn[:, p * LANES:(p + 1) * LANES] * qg_ref[...]
        q_ref[p] = (_rope(qp, rc, rs1, rs2) * (SWA_HEAD_DIM ** -0.5)).astype(BF16)

    kv = mm(C_KV, C_KV + 2 * SWA_KV)
    kf = kv[:, :SWA_KV]
    ssk = jnp.dot((kf * kf).astype(BF16), e_ref[0:SWA_KV, 0:SWA_KV], preferred_element_type=F32)
    kn = kf * lax.rsqrt(ssk * inv_hd + EPS) * kg_ref[...]
    k_ref[...] = _rope(kn, rc, rs1, rs2)
    v_ref[...] = kv[:, SWA_KV:]

    xb_ref[...] = mm(C_XB, C_XB + LRU_WIDTH).astype(BF16)
    gy_ref[...] = _gelu_tanh(mm(C_YB, C_YB + LRU_WIDTH)).astype(BF16)

    qk = mm(C_GQK, C_GQK + 2 * GLA_K)
    gq_ref[...] = qk[:, :GLA_K] * (GLA_DK ** -0.5)
    gk_ref[...] = qk[:, GLA_K:]
    gv_ref[...] = mm(C_GV, C_GV + GLA_V).astype(BF16)
    sr_ref[...] = _silu(mm(C_GR, C_GR + GLA_V)).astype(BF16)

    ac = mm(C_AC, C_AC + LANES).astype(BF16)
    z = jnp.dot(ac, wa2_ref[...], preferred_element_type=F32) + ba_ref[...]
    la_ref[...] = -_softplus(-z) * (1.0 / GLA_TAU)

    for c in range(3 * D_MODEL // 512):
        gate_ref[:, c * 512:(c + 1) * 512] = _sigmoid(
            mm(C_GATE + c * 512, C_GATE + (c + 1) * 512)).astype(BF16)


def _in_proj(x, w, rope_tabs, tm):
    m = x.shape[0]
    rc, rs1, rs2 = rope_tabs
    ntab = rc.shape[0] // tm
    tab_spec = pl.BlockSpec((tm, LANES), lambda i: (i % ntab, 0))
    outs = [
        jax.ShapeDtypeStruct((SWA_Q // LANES, m, LANES), BF16),
        jax.ShapeDtypeStruct((m, SWA_KV), F32),
        jax.ShapeDtypeStruct((m, SWA_KV), F32),
        jax.ShapeDtypeStruct((m, LRU_WIDTH), BF16),
        jax.ShapeDtypeStruct((m, LRU_WIDTH), BF16),
        jax.ShapeDtypeStruct((m, GLA_K), F32),
        jax.ShapeDtypeStruct((m, GLA_K), F32),
        jax.ShapeDtypeStruct((m, GLA_V), BF16),
        jax.ShapeDtypeStruct((m, GLA_V), BF16),
        jax.ShapeDtypeStruct((m, GLA_K), F32),
        jax.ShapeDtypeStruct((m, 3 * D_MODEL), BF16),
    ]
    out_specs = [
        pl.BlockSpec((SWA_Q // LANES, tm, LANES), lambda i: (0, i, 0)),
        _rows(tm, SWA_KV), _rows(tm, SWA_KV), _rows(tm, LRU_WIDTH), _rows(tm, LRU_WIDTH),
        _rows(tm, GLA_K), _rows(tm, GLA_K), _rows(tm, GLA_V), _rows(tm, GLA_V), _rows(tm, GLA_K),
        _rows(tm, 3 * D_MODEL),
    ]
    in_specs = [
        _rows(tm, D_MODEL), _full((1, D_MODEL)), _full((D_MODEL, IN_COLS_PAD)),
        _full((SWA_Q, SWA_Q)), _full((1, LANES)), _full((1, LANES)),
        tab_spec, tab_spec, tab_spec,
        _full((LANES, GLA_K)), _full((1, GLA_K)),
    ]
    return pl.pallas_call(
        _in_proj_kernel, grid=(m // tm,), in_specs=in_specs, out_specs=out_specs, out_shape=outs,
        compiler_params=_cparams(("parallel",)), name="in_proj",
    )(x, w["norm_mix_g"], w["w_in"], w["e_swa"], w["swa_qn_g"], w["swa_kn_g"], rc, rs1, rs2,
      w["gla_wa2"], w["gla_ba"])


def _sink_softmax(s, sk):
    m = jnp.maximum(jnp.max(s, axis=-1, keepdims=True), sk)
    e = jnp.exp(s - m)
    den = jnp.sum(e, axis=-1, keepdims=True) + jnp.exp(sk - m)
    return e / den


def _swa_prompt_kernel(q_ref, kc_ref, kp_ref, vc_ref, vp_ref, sink_ref, o_ref, *, nblk):
    j = pl.program_id(1)
    w = SWA_WINDOW
    lane = lax.broadcasted_iota(jnp.int32, (1, LANES), 1)
    lo = lane < SWA_HEAD_DIM

    def dup(x, kvh):
        r = pltpu.roll(x, SWA_HEAD_DIM, 1)
        y = jnp.where(lo, x, r) if kvh == 0 else jnp.where(lo, r, x)
        return y.astype(BF16)

    kall = jnp.concatenate([kp_ref[...], kc_ref[...]], axis=0)
    vall = jnp.concatenate([vp_ref[...], vc_ref[...]], axis=0)
    kd = [dup(kall, 0), dup(kall, 1)]
    vd = [dup(vall, 0), dup(vall, 1)]

    qi = lax.broadcasted_iota(jnp.int32, (w, 2 * w), 0) + w
    ki = lax.broadcasted_iota(jnp.int32, (w, 2 * w), 1)
    rel = qi - ki
    band = (rel >= 0) & (rel <= w)
    first = band & ((ki >= w) | ((jnp.zeros_like(ki) + j) > 0))
    row = lax.broadcasted_iota(jnp.int32, (2 * w, 1), 0)

    for c in range(nblk):
        ok = first if c == 0 else band
        ok2 = jnp.concatenate([ok, ok], axis=0)
        for kvh in range(SWA_KV_HEADS):
            kk = kd[kvh][c * w:(c + 2) * w]
            vv = vd[kvh][c * w:(c + 2) * w]
            qg = q_ref[2 * kvh:2 * kvh + 2, c * w:(c + 1) * w, :].reshape(2 * w, LANES)
            res = []
            for par in range(2):
                qm = jnp.where(lo if par == 0 else jnp.logical_not(lo), qg, jnp.zeros_like(qg))
                s = lax.dot_general(qm, kk, (((1,), (1,)), ((), ())), preferred_element_type=F32)
                s = jnp.where(ok2, s, NEG)
                h0 = kvh * 4 + par
                sk = jnp.where(row < w, sink_ref[h0], sink_ref[h0 + 2])
                p = _sink_softmax(s, sk).astype(BF16)
                res.append(jnp.dot(p, vv, preferred_element_type=F32))
            oo = jnp.where(lo, res[0], res[1]).astype(BF16)
            o_ref[2 * kvh:2 * kvh + 2, c * w:(c + 1) * w, :] = oo.reshape(2, w, LANES)


def _swa_prompt(q4, k, v, sink, b, t):
    m = b * t
    w = SWA_WINDOW
    tq = min(512, t)
    nj = t // tq
    nblk = tq // w
    r = t // w
    cur = lambda bb, j: (bb * nj + j, 0)
    prev = lambda bb, j: (jnp.maximum(bb * r + j * nblk - 1, 0), 0)
    qspec = pl.BlockSpec((4, tq, LANES), lambda bb, j: (0, bb * nj + j, 0))
    return pl.pallas_call(
        functools.partial(_swa_prompt_kernel, nblk=nblk),
        grid=(b, nj),
        in_specs=[qspec, pl.BlockSpec((tq, LANES), cur), pl.BlockSpec((w, LANES), prev),
                  pl.BlockSpec((tq, LANES), cur), pl.BlockSpec((w, LANES), prev),
                  pl.BlockSpec(memory_space=pltpu.SMEM)],
        out_specs=qspec,
        out_shape=jax.ShapeDtypeStruct((4, m, LANES), BF16),
        compiler_params=_cparams(("parallel", "parallel")), name="swa_prompt",
    )(q4, k, k, v, v, sink)


def _swa_step_kernel(q_ref, kc_ref, kn_ref, vc_ref, vn_ref, sk_ref, o_ref, *, bb, ts):
    w = SWA_WINDOW
    nk = w + 8
    row_t = lax.broadcasted_iota(jnp.int32, (SWA_HEADS * ts, nk), 0) % ts
    key = lax.broadcasted_iota(jnp.int32, (SWA_HEADS * ts, nk), 1)
    ok = ((key < w) & (key >= row_t)) | ((key >= w) & ((key - w) <= row_t))
    sk = sk_ref[...]
    for i in range(bb):
        kk = jnp.concatenate([kc_ref[i], kn_ref[i]], axis=0).astype(BF16)
        vv = jnp.concatenate([vc_ref[i], vn_ref[i]], axis=0).astype(BF16)
        s = lax.dot_general(q_ref[i], kk, (((1,), (1,)), ((), ())), preferred_element_type=F32)
        s = jnp.where(ok, s, NEG)
        p = _sink_softmax(s, sk).astype(BF16)
        o_ref[i] = jnp.dot(p, vv, preferred_element_type=F32)


def _swa_step(qm, kc, kn8, vc, vn8, sink_rows, ts):
    b = qm.shape[0]
    bb = 8
    nq = SWA_HEADS * ts
    w = SWA_WINDOW
    blk = lambda r: pl.BlockSpec((bb, r, LANES), lambda i: (i, 0, 0))
    return pl.pallas_call(
        functools.partial(_swa_step_kernel, bb=bb, ts=ts),
        grid=(b // bb,),
        in_specs=[blk(nq), blk(w), blk(8), blk(w), blk(8), _full((nq, 1))],
        out_specs=blk(nq),
        out_shape=jax.ShapeDtypeStruct((b, nq, LANES), F32),
        compiler_params=_cparams(("parallel",)), name="swa_step",
    )(qm, kc, kn8, vc, vn8, sink_rows)


def _hist_rows(s, kw):
    return 8 if s == 1 else (kw - 1) * s


def _conv_taps(x, hist, cw_rows, bias, *, s, tm, kw):
    y = bias + x * cw_rows[kw - 1:kw]
    row8 = lax.broadcasted_iota(jnp.int32, (8, 1), 0)
    for k in range(1, kw):
        d = k * s
        if s == 1:
            sh = pltpu.roll(x, d, 0)
            top = jnp.where(row8 < d, pltpu.roll(hist, d, 0), sh[:8])
            sh = jnp.concatenate([top, sh[8:]], axis=0)
        else:
            sh = jnp.concatenate([hist[hist.shape[0] - d:], x[:tm - d]], axis=0)
        y = y + sh * cw_rows[kw - 1 - k:kw - k]
    return y, x[tm - _hist_rows(s, kw):]


def _combine_shifted(a, b, d, keep, axis):
    a_sh = jnp.where(keep, pltpu.roll(a, d, axis), 1.0)
    b_sh = jnp.where(keep, pltpu.roll(b, d, axis), 0.0)
    return a * a_sh, a * b_sh + b


def _scan_adjacent_rows(a, b, hprev, a_sc, b_sc, hin_sc, tm):
    c = a.shape[1]
    ng = tm // 8
    nl = c // LANES
    a3 = a.reshape(ng, 8, c)
    b3 = b.reshape(ng, 8, c)
    r8 = lax.broadcasted_iota(jnp.int32, (1, 8, 1), 1)
    d = 1
    while d < 8:
        a3, b3 = _combine_shifted(a3, b3, d, r8 >= d, 1)
        d *= 2
    a2 = a3.reshape(tm, c)
    b2 = b3.reshape(tm, c)
    for l in range(nl):
        a_sc[l] = a2[:, l * LANES:(l + 1) * LANES]
        b_sc[l] = b2[:, l * LANES:(l + 1) * LANES]
    last = pl.ds(7, ng, stride=8)
    ag = jnp.concatenate([a_sc[l, last, :] for l in range(nl)], axis=1)
    bg = jnp.concatenate([b_sc[l, last, :] for l in range(nl)], axis=1)
    rg = lax.broadcasted_iota(jnp.int32, (ng, 1), 0)
    d = 1
    while d < ng:
        ag, bg = _combine_shifted(ag, bg, d, rg >= d, 0)
        d *= 2
    hg = ag * hprev + bg
    hin_sc[...] = jnp.where(rg >= 1, pltpu.roll(hg, 1, 0), hprev)
    return jnp.concatenate(
        [a2[g * 8:(g + 1) * 8] * hin_sc[g:g + 1, :] + b2[g * 8:(g + 1) * 8] for g in range(ng)],
        axis=0)


def _lru_kernel(xb_ref, gy_ref, hist_ref, h0_ref, cw_ref, cb_ref, wa_ref, ba_ref, wx_ref, bx_ref,
                lam_ref, ob_ref, hl_ref, co_ref, cx_sc, ch_sc, *scan_sc, s, tm):
    j = pl.program_id(1)
    kw = LRU_CONV_W

    @pl.when(j == 0)
    def _():
        cx_sc[...] = hist_ref[...]
        ch_sc[...] = h0_ref[...]

    x = xb_ref[...].astype(F32)
    xc, nh = _conv_taps(x, cx_sc[...], cw_ref[...], cb_ref[...], s=s, tm=tm, kw=kw)
    cx_sc[...] = nh
    co_ref[...] = nh

    xcb = xc.astype(BF16)
    r_g = _sigmoid(jnp.dot(xcb, wa_ref[...], preferred_element_type=F32) + ba_ref[...])
    i_g = _sigmoid(jnp.dot(xcb, wx_ref[...], preferred_element_type=F32) + bx_ref[...])
    decay = (-LRU_C) * _softplus(-lam_ref[...])
    a = jnp.exp(r_g * decay)
    bv = jnp.sqrt(_neg_expm1(r_g * (2.0 * decay))) * (i_g * xc)

    if s == 1:
        h = _scan_adjacent_rows(a, bv, ch_sc[...], *scan_sc, tm)
        hlast = h[tm - 8:, :][7:, :]
    else:
        row = lax.broadcasted_iota(jnp.int32, (tm, 1), 0)
        h = bv + a * jnp.concatenate([ch_sc[...], jnp.zeros((tm - s, LRU_WIDTH), F32)], axis=0)
        d = s
        while d < tm:
            a, h = _combine_shifted(a, h, d, row >= d, 0)
            d *= 2
        hlast = h[tm - s:, :]
    ch_sc[...] = hlast
    hl_ref[...] = hlast
    ob_ref[...] = (gy_ref[...].astype(F32) * h).astype(BF16)


def _lru(xb, gy, hist, h0, w, s, tm, ngroups):
    m = xb.shape[0]
    nj = m // (ngroups * tm)
    kw = LRU_CONV_W
    hr = _hist_rows(s, kw)
    rowspec = pl.BlockSpec((tm, LRU_WIDTH), lambda g, j: (g * nj + j, 0))
    gspec = lambda r: pl.BlockSpec((None, r, LRU_WIDTH), lambda g, j: (g, 0, 0))
    full2 = lambda shape: pl.BlockSpec(shape, lambda g, j: (0, 0))
    scratch = [pltpu.VMEM((hr, LRU_WIDTH), F32), pltpu.VMEM((s, LRU_WIDTH), F32)]
    if s == 1:
        scratch += [pltpu.VMEM((LRU_WIDTH // LANES, tm, LANES), F32),
                    pltpu.VMEM((LRU_WIDTH // LANES, tm, LANES), F32),
                    pltpu.VMEM((tm // 8, LRU_WIDTH), F32)]
    return pl.pallas_call(
        functools.partial(_lru_kernel, s=s, tm=tm),
        grid=(ngroups, nj),
        in_specs=[rowspec, rowspec, gspec(hr), gspec(s),
                  full2((kw, LRU_WIDTH)), full2((1, LRU_WIDTH)),
                  full2((LRU_WIDTH, LRU_WIDTH)), full2((1, LRU_WIDTH)),
                  full2((LRU_WIDTH, LRU_WIDTH)), full2((1, LRU_WIDTH)), full2((1, LRU_WIDTH))],
        out_specs=[rowspec, gspec(s), gspec(hr)],
        out_shape=[jax.ShapeDtypeStruct((m, LRU_WIDTH), BF16),
                   jax.ShapeDtypeStruct((ngroups, s, LRU_WIDTH), F32),
                   jax.ShapeDtypeStruct((ngroups, hr, LRU_WIDTH), F32)],
        scratch_shapes=scratch,
        compiler_params=_cparams(("arbitrary", "arbitrary")), name="lru",
    )(xb, gy, hist, h0, w["lru_conv_w"], w["lru_conv_b"], w["lru_wa"], w["lru_ba"], w["lru_wx"],
      w["lru_bx"], w["lru_lambda"])


def _gla_prompt_kernel(q_ref, k_ref, v_ref, sr_ref, la_ref, g_ref, o_ref, st_ref, s_sc, *, tm):
    j = pl.program_id(1)
    c32 = GLA_CHUNK
    nc = tm // c32

    @pl.when(j == 0)
    def _():
        s_sc[...] = jnp.zeros_like(s_sc)

    row = lax.broadcasted_iota(jnp.int32, (tm, 1), 0)
    rin = row % c32
    cum = la_ref[...]
    d = 1
    while d < c32:
        cum = cum + jnp.where(rin >= d, pltpu.roll(cum, d, 0), 0.0)
        d *= 2
    cum3 = cum.reshape(nc, c32, GLA_K)
    last3 = cum3[:, c32 - 1:c32, :]
    k_all = k_ref[...]
    qt = q_ref[...] * jnp.exp(cum)
    kt = (k_all * jnp.exp(-cum)).astype(BF16)
    kl = (k_all.reshape(nc, c32, GLA_K) * jnp.exp(last3 - cum3)).reshape(tm, GLA_K)
    dec = jnp.exp(last3)

    ti = lax.broadcasted_iota(jnp.int32, (tm, tm), 0)
    si = lax.broadcasted_iota(jnp.int32, (tm, tm), 1)
    causal = (si <= ti) & ((ti // c32) == (si // c32))
    lane = lax.broadcasted_iota(jnp.int32, (1, LANES), 1)
    lo = lane < GLA_DK
    rch = row // c32

    def by_chunk(x):
        return jnp.concatenate([jnp.where(rch == c, x, 0.0).astype(BF16) for c in range(nc)], axis=1)

    for p in range(GLA_HEADS // 2):
        sl = slice(p * LANES, (p + 1) * LANES)
        qtp = qt[:, sl]
        ktp = kt[:, sl]
        kl_x = by_chunk(kl[:, sl])
        for par in range(2):
            h = 2 * p + par
            hs = slice(h * GLA_DV, (h + 1) * GLA_DV)
            qmf = jnp.where(lo if par == 0 else jnp.logical_not(lo), qtp, 0.0)
            qm = qmf.astype(BF16)
            att = lax.dot_general(qm, ktp, (((1,), (1,)), ((), ())), preferred_element_type=F32)
            att = jnp.where(causal, att, 0.0).astype(BF16)
            vh = v_ref[:, hs]
            o_intra = jnp.dot(att, vh, preferred_element_type=F32)
            ut = lax.dot_general(vh, kl_x, (((0,), (0,)), ((), ())), preferred_element_type=F32)
            st = s_sc[h]
            sts = []
            for c in range(nc):
                sts.append(st.astype(BF16))
                st = st * dec[c][:, sl] + ut[:, c * LANES:(c + 1) * LANES]
            s_sc[h] = st
            o_inter = lax.dot_general(by_chunk(qmf), jnp.concatenate(sts, axis=1),
                                      (((1,), (1,)), ((), ())), preferred_element_type=F32)
            o = o_intra + o_inter
            on = _rms(o, g_ref[...])
            o_ref[:, hs] = (on * sr_ref[:, hs].astype(F32)).astype(BF16)
    st_ref[...] = s_sc[...]


def _gla_prompt(gq, gk, gv, sr, la, on_g, b, t):
    m = b * t
    tm = min(256, t)
    nj = t // tm
    rs = lambda c: pl.BlockSpec((tm, c), lambda bb, j: (bb * nj + j, 0))
    sspec = pl.BlockSpec((None, GLA_HEADS, GLA_DV, LANES), lambda bb, j: (bb, 0, 0, 0))
    return pl.pallas_call(
        functools.partial(_gla_prompt_kernel, tm=tm),
        grid=(b, nj),
        in_specs=[rs(GLA_K), rs(GLA_K), rs(GLA_V), rs(GLA_V), rs(GLA_K),
                  pl.BlockSpec((1, GLA_DV), lambda bb, j: (0, 0))],
        out_specs=[rs(GLA_V), sspec],
        out_shape=[jax.ShapeDtypeStruct((m, GLA_V), BF16),
                   jax.ShapeDtypeStruct((b, GLA_HEADS, GLA_DV, LANES), F32)],
        scratch_shapes=[pltpu.VMEM((GLA_HEADS, GLA_DV, LANES), F32)],
        compiler_params=_cparams(("arbitrary", "arbitrary")), name="gla_prompt",
    )(gq, gk, gv, sr, la, on_g)


def _gla_step_kernel(q_ref, k_ref, v_ref, sr_ref, la_ref, s0_ref, g_ref, o_ref, sn_ref, *, bb, ts):
    row = lax.broadcasted_iota(jnp.int32, (8, 1), 0)
    ti = lax.broadcasted_iota(jnp.int32, (8, 8), 0)
    si = lax.broadcasted_iota(jnp.int32, (8, 8), 1)
    causal = si <= ti
    lane = lax.broadcasted_iota(jnp.int32, (1, LANES), 1)
    lo = lane < GLA_DK
    for i in range(bb):
        cum = la_ref[i]
        d = 1
        while d < ts:
            cum = cum + jnp.where(row >= d, pltpu.roll(cum, d, 0), 0.0)
            d *= 2
        last = cum[ts - 1:ts, :]
        k_all = k_ref[i]
        qt = q_ref[i] * jnp.exp(cum)
        kt = (k_all * jnp.exp(-cum)).astype(BF16)
        kl = (k_all * jnp.exp(last - cum)).astype(BF16)
        dec = jnp.exp(last)
        for p in range(GLA_HEADS // 2):
            sl = slice(p * LANES, (p + 1) * LANES)
            sp = s0_ref[i, p]
            dcol = jnp.broadcast_to(dec[:, sl], (LANES, LANES)).T
            spb = sp.astype(BF16)
            klp = kl[:, sl]
            new_rows = []
            for par in range(2):
                h = 2 * p + par
                hs = slice(h * GLA_DV, (h + 1) * GLA_DV)
                qm = jnp.where(lo if par == 0 else jnp.logical_not(lo), qt[:, sl], 0.0).astype(BF16)
                att = lax.dot_general(qm, kt[:, sl], (((1,), (1,)), ((), ())),
                                      preferred_element_type=F32)
                att = jnp.where(causal, att, 0.0).astype(BF16)
                vh = v_ref[i][:, hs]
                o = jnp.dot(att, vh, preferred_element_type=F32) + jnp.dot(
                    qm, spb, preferred_element_type=F32)
                on = _rms(o, g_ref[...])
                o_ref[i, :, hs] = (on * sr_ref[i][:, hs].astype(F32)).astype(BF16)
                u = lax.dot_general(klp, vh, (((0,), (0,)), ((), ())), preferred_element_type=F32)
                new_rows.append(u[par * GLA_DK:(par + 1) * GLA_DK, :])
            sn_ref[i, p] = sp * dcol + jnp.concatenate(new_rows, axis=0)


def _gla_step(gq8, gk8, gv8, sr8, la8, s0p, on_g, ts):
    b = gq8.shape[0]
    bb = 8
    blk = lambda c: pl.BlockSpec((bb, 8, c), lambda i: (i, 0, 0))
    sspec = pl.BlockSpec((bb, GLA_HEADS // 2, LANES, GLA_DV), lambda i: (i, 0, 0, 0))
    return pl.pallas_call(
        functools.partial(_gla_step_kernel, bb=bb, ts=ts),
        grid=(b // bb,),
        in_specs=[blk(GLA_K), blk(GLA_K), blk(GLA_V), blk(GLA_V), blk(GLA_K), sspec,
                  _full((1, GLA_DV))],
        out_specs=[blk(GLA_V), sspec],
        out_shape=[jax.ShapeDtypeStruct((b, 8, GLA_V), BF16),
                   jax.ShapeDtypeStruct((b, GLA_HEADS // 2, LANES, GLA_DV), F32)],
        compiler_params=_cparams(("parallel",)), name="gla_step",
    )(gq8, gk8, gv8, sr8, la8, s0p, on_g)


def _merge_kernel(oa_ref, ob_ref, oc_ref, gate_ref, x_ref, wa_ref, wb_ref, wc_ref, wo_ref,
                  nx_ref, wq_ref, qn_ref, x1_ref, qx_ref):
    oa = jnp.concatenate([oa_ref[p] for p in range(SWA_Q // LANES)], axis=-1)
    dm = D_MODEL
    merged = (gate_ref[:, 0:dm].astype(F32) * jnp.dot(oa, wa_ref[...], preferred_element_type=F32)
              + gate_ref[:, dm:2 * dm].astype(F32)
              * jnp.dot(ob_ref[...], wb_ref[...], preferred_element_type=F32)
              + gate_ref[:, 2 * dm:3 * dm].astype(F32)
              * jnp.dot(oc_ref[...], wc_ref[...], preferred_element_type=F32))
    x1 = x_ref[...] + jnp.dot(merged.astype(BF16), wo_ref[...], preferred_element_type=F32)
    x1_ref[...] = x1
    hx = _rms(x1, nx_ref[...]).astype(BF16)
    q = jnp.dot(hx, wq_ref[...], preferred_element_type=F32)
    for h in range(MEM_HEADS):
        hs = slice(h * MEM_HEAD_DIM, (h + 1) * MEM_HEAD_DIM)
        qx_ref[:, hs] = _rms(q[:, hs], qn_ref[...]).astype(BF16)


def _merge(oa4, ob, oc, gates, x, w, tm):
    m = x.shape[0]
    return pl.pallas_call(
        _merge_kernel, grid=(m // tm,),
        in_specs=[pl.BlockSpec((SWA_Q // LANES, tm, LANES), lambda i: (0, i, 0)),
                  _rows(tm, LRU_WIDTH), _rows(tm, GLA_V), _rows(tm, 3 * D_MODEL), _rows(tm, D_MODEL),
                  _full((SWA_Q, D_MODEL)), _full((LRU_WIDTH, D_MODEL)), _full((GLA_V, D_MODEL)),
                  _full((D_MODEL, D_MODEL)), _full((1, D_MODEL)), _full((D_MODEL, MEM_W)),
                  _full((1, MEM_HEAD_DIM))],
        out_specs=[_rows(tm, D_MODEL), _rows(tm, MEM_W)],
        out_shape=[jax.ShapeDtypeStruct((m, D_MODEL), F32), jax.ShapeDtypeStruct((m, MEM_W), BF16)],
        compiler_params=_cparams(("parallel",)), name="merge",
    )(oa4, ob, oc, gates, x, w["w_branch_a"], w["w_branch_b"], w["w_branch_c"], w["w_out"],
      w["norm_x_g"], w["x_wq"], w["x_qn_g"])


def _memkv_kernel(m_ref, g_ref, wk_ref, wv_ref, kn_ref, k_ref, v_ref):
    hm = _rms(m_ref[...], g_ref[...]).astype(BF16)
    k = jnp.dot(hm, wk_ref[...], preferred_element_type=F32)
    for h in range(MEM_HEADS):
        hs = slice(h * MEM_HEAD_DIM, (h + 1) * MEM_HEAD_DIM)
        k_ref[:, hs] = _rms(k[:, hs], kn_ref[...])
    v_ref[...] = jnp.dot(hm, wv_ref[...], preferred_element_type=F32)


def _memkv(mem, w, tm):
    m = mem.shape[0]
    return pl.pallas_call(
        _memkv_kernel, grid=(m // tm,),
        in_specs=[_rows(tm, D_MODEL), _full((1, D_MODEL)), _full((D_MODEL, MEM_W)),
                  _full((D_MODEL, MEM_W)), _full((1, MEM_HEAD_DIM))],
        out_specs=[_rows(tm, MEM_W), _rows(tm, MEM_W)],
        out_shape=[jax.ShapeDtypeStruct((m, MEM_W), F32), jax.ShapeDtypeStruct((m, MEM_W), F32)],
        compiler_params=_cparams(("parallel",)), name="memkv",
    )(mem, w["norm_mem_g"], w["x_wk"], w["x_wv"], w["x_kn_g"])


def _xattn_heads(q, mk, mv):
    outs = []
    for h in range(MEM_HEADS):
        hs = slice(h * MEM_HEAD_DIM, (h + 1) * MEM_HEAD_DIM)
        s = lax.dot_general(q[:, hs], mk[:, hs].astype(BF16), (((1,), (1,)), ((), ())),
                            preferred_element_type=F32) * (MEM_HEAD_DIM ** -0.5)
        m = jnp.max(s, axis=-1, keepdims=True)
        e = jnp.exp(s - m)
        p = (e / jnp.sum(e, axis=-1, keepdims=True)).astype(BF16)
        outs.append(jnp.dot(p, mv[:, hs].astype(BF16), preferred_element_type=F32))
    return jnp.concatenate(outs, axis=-1)


def _xattn_prompt_kernel(q_ref, mk_ref, mv_ref, o_ref):
    o_ref[...] = _xattn_heads(q_ref[...], mk_ref[...], mv_ref[...]).astype(BF16)


def _xattn_prompt(qx, mk, mv, b, t):
    m = b * t
    n_mem = mk.shape[0] // b
    tq = min(512, t)
    nj = t // tq
    qspec = pl.BlockSpec((tq, MEM_W), lambda bb, j: (bb * nj + j, 0))
    mspec = pl.BlockSpec((n_mem, MEM_W), lambda bb, j: (bb, 0))
    return pl.pallas_call(
        _xattn_prompt_kernel, grid=(b, nj), in_specs=[qspec, mspec, mspec], out_specs=qspec,
        out_shape=jax.ShapeDtypeStruct((m, MEM_W), BF16),
        compiler_params=_cparams(("parallel", "parallel")), name="xattn_prompt",
    )(qx, mk, mv)


def _xattn_step_kernel(q_ref, mk_ref, mv_ref, o_ref, *, bb):
    def heads_on_lanes(ref, i):
        return jnp.concatenate([ref[i, :, h, :] for h in range(MEM_HEADS)], axis=1).astype(BF16)

    for i in range(bb):
        s = lax.dot_general(q_ref[i], heads_on_lanes(mk_ref, i), (((1,), (1,)), ((), ())),
                            preferred_element_type=F32) * (MEM_HEAD_DIM ** -0.5)
        m = jnp.max(s, axis=-1, keepdims=True)
        e = jnp.exp(s - m)
        p = (e / jnp.sum(e, axis=-1, keepdims=True)).astype(BF16)
        o_ref[i] = jnp.dot(p, heads_on_lanes(mv_ref, i), preferred_element_type=F32)


def _xattn_step(qexp, mk_all, mv_all, layer):
    b, nq = qexp.shape[0], qexp.shape[1]
    n_mem = mk_all.shape[2]
    bb = 8
    qspec = pl.BlockSpec((bb, nq, MEM_W), lambda i: (i, 0, 0))
    mspec = pl.BlockSpec((None, bb, n_mem, MEM_HEADS, MEM_HEAD_DIM), lambda i: (layer, i, 0, 0, 0))
    return pl.pallas_call(
        functools.partial(_xattn_step_kernel, bb=bb), grid=(b // bb,),
        in_specs=[qspec, mspec, mspec], out_specs=qspec,
        out_shape=jax.ShapeDtypeStruct((b, nq, MEM_W), F32),
        compiler_params=_cparams(("parallel",)), name="xattn_step",
    )(qexp, mk_all, mv_all)


def _ffn_kernel(*refs, s, tm, attn):
    if attn:
        q_ref, mk_ref, mv_ref, *refs = refs
        ox = _xattn_heads(q_ref[...], mk_ref[...], mv_ref[...]).astype(BF16)
    else:
        ox_ref, *refs = refs
        ox = ox_ref[...]
    (x_ref, hist_ref, wo_ref, g_ref, wu_ref, cw_ref, cb_ref, wd_ref, x3_ref, co_ref, act_sc, *hs) = refs
    kw = FFN_CONV_W
    cwid = 256
    if hs:
        hs_sc = hs[0]

        @pl.when(pl.program_id(1) == 0)
        def _():
            hs_sc[...] = hist_ref[...]
    else:
        hs_sc = hist_ref

    x2 = x_ref[...] + jnp.dot(ox, wo_ref[...], preferred_element_type=F32)
    x3_ref[...] = x2
    hf = _rms(x2, g_ref[...]).astype(BF16)
    for c in range(D_FF // cwid):
        ys = []
        for c0 in (c * cwid, D_FF + c * cwid):
            cs = slice(c0, c0 + cwid)
            u = jnp.dot(hf, wu_ref[:, cs], preferred_element_type=F32)
            y, nh = _conv_taps(u, hs_sc[:, cs], cw_ref[:, cs], cb_ref[:, cs], s=s, tm=tm, kw=kw)
            if hs:
                hs_sc[:, cs] = nh
            co_ref[:, cs] = nh
            ys.append(y)
        act_sc[:, c * cwid:(c + 1) * cwid] = (_silu(ys[0]) * ys[1]).astype(BF16)
    x3_ref[...] = x3_ref[...] + jnp.dot(act_sc[...], wd_ref[...], preferred_element_type=F32)


def _ffn(ox, x1, hist, w, s, tm, ngroups, mem=None):
    m = x1.shape[0]
    nj = m // (ngroups * tm)
    hr = _hist_rows(s, FFN_CONV_W)
    rs = lambda c: pl.BlockSpec((tm, c), lambda g, j: (g * nj + j, 0))
    hspec = pl.BlockSpec((None, hr, 2 * D_FF), lambda g, j: (g, 0, 0), pipeline_mode=pl.Buffered(1))
    ospec = pl.BlockSpec((None, hr, 2 * D_FF), lambda g, j: (g, 0, 0))
    wfull = lambda shape: pl.BlockSpec(shape, lambda g, j: (0, 0), pipeline_mode=pl.Buffered(1))
    scratch = [pltpu.VMEM((tm, D_FF), BF16)]
    if nj > 1:
        scratch.append(pltpu.VMEM((hr, 2 * D_FF), F32))
    mem_specs, mem_args = [], ()
    if mem is not None:
        n_mem = mem[0].shape[0] // ngroups
        mem_specs = [pl.BlockSpec((n_mem, MEM_W), lambda g, j: (g, 0))] * 2
        mem_args = tuple(mem)
    return pl.pallas_call(
        functools.partial(_ffn_kernel, s=s, tm=tm, attn=mem is not None),
        grid=(ngroups, nj),
        in_specs=[rs(MEM_W)] + mem_specs + [
                  rs(D_MODEL), hspec, wfull((MEM_W, D_MODEL)), wfull((1, D_MODEL)),
                  wfull((D_MODEL, 2 * D_FF)), wfull((FFN_CONV_W, 2 * D_FF)), wfull((1, 2 * D_FF)),
                  wfull((D_FF, D_MODEL))],
        out_specs=[rs(D_MODEL), ospec],
        out_shape=[jax.ShapeDtypeStruct((m, D_MODEL), F32),
                   jax.ShapeDtypeStruct((ngroups, hr, 2 * D_FF), F32)],
        scratch_shapes=scratch,
        compiler_params=_cparams(("arbitrary", "arbitrary")), name="ffn",
    )(ox, *mem_args, x1, hist, w["x_wo"], w["norm_ffn_g"], w["ffn_w_up"], w["ffn_conv_w"],
      w["ffn_conv_b"], w["ffn_w_down"])


def _rope_tables(pos):
    half = ROPE_DIM // 2
    n = pos.shape[0]
    inv = ROPE_THETA ** (-jnp.arange(half, dtype=F32) * 2.0 / ROPE_DIM)
    ang = pos.astype(F32)[:, None] * inv[None, :]
    cos, sin = jnp.cos(ang), jnp.sin(ang)
    rest = SWA_HEAD_DIM - ROPE_DIM
    z8 = jnp.zeros((n, half), F32)
    zr = jnp.zeros((n, rest), F32)
    rc = jnp.concatenate([cos, cos, jnp.ones((n, rest), F32)], axis=1)
    rs1 = jnp.concatenate([-sin, z8, zr], axis=1)
    rs2 = jnp.concatenate([z8, sin, zr], axis=1)
    rep = LANES // SWA_HEAD_DIM
    return tuple(jnp.tile(a, (1, rep)) for a in (rc, rs1, rs2))


def _block_diag(w):
    n, bi, bj = w.shape
    eye = jnp.eye(n, dtype=w.dtype)
    return (w[:, :, None, :] * eye[:, None, :, None]).reshape(n * bi, n * bj)


def _layer_weights(p, l):
    w_in = p["w_in"][l]
    split = 3344
    w_in_p = jnp.concatenate(
        [w_in[:, :split], jnp.zeros((D_MODEL, AC_PAD), F32), w_in[:, split:]], axis=1).astype(BF16)
    wa2 = jnp.concatenate([p["gla_wa2"][l], jnp.zeros((AC_PAD, GLA_K), F32)], axis=0).astype(BF16)
    row = lambda a: a[l][None, :]
    return dict(
        norm_mix_g=row(p["norm_mix_g"]), w_in=w_in_p, e_swa=_block_ones(SWA_HEADS, SWA_HEAD_DIM),
        swa_qn_g=jnp.tile(p["swa_qn_g"][l], LANES // SWA_HEAD_DIM)[None, :],
        swa_kn_g=jnp.tile(p["swa_kn_g"][l], LANES // SWA_HEAD_DIM)[None, :],
        gla_wa2=wa2, gla_ba=row(p["gla_ba"]),
        lru_conv_w=p["lru_conv_w"][l], lru_conv_b=row(p["lru_conv_b"]),
        lru_wa=_block_diag(p["lru_wa"][l]).astype(BF16), lru_ba=row(p["lru_ba"]),
        lru_wx=_block_diag(p["lru_wx"][l]).astype(BF16), lru_bx=row(p["lru_bx"]),
        lru_lambda=row(p["lru_lambda"]), gla_on_g=row(p["gla_on_g"]),
        w_branch_a=p["w_branch_a"][l].astype(BF16), w_branch_b=p["w_branch_b"][l].astype(BF16),
        w_branch_c=p["w_branch_c"][l].astype(BF16), w_out=p["w_out"][l].astype(BF16),
        norm_x_g=row(p["norm_x_g"]), norm_mem_g=row(p["norm_mem_g"]),
        x_wq=p["x_wq"][l].astype(BF16), x_wk=p["x_wk"][l].astype(BF16),
        x_wv=p["x_wv"][l].astype(BF16), x_qn_g=row(p["x_qn_g"]), x_kn_g=row(p["x_kn_g"]),
        x_wo=p["x_wo"][l].astype(BF16), norm_ffn_g=row(p["norm_ffn_g"]),
        ffn_w_up=p["ffn_w_up"][l].astype(BF16), ffn_conv_w=p["ffn_conv_w"][l],
        ffn_conv_b=row(p["ffn_conv_b"]), ffn_w_down=p["ffn_w_down"][l].astype(BF16),
        swa_sink=p["swa_sink"][l],
    )


def _prompt_layer(x, mem, w, tabs, b, t):
    tm = min(TM_PROMPT, t)
    n_mem = mem.shape[0] // b
    (q4, k, v, xb, gy, gq, gk, gv, sr, la, gates) = _in_proj(x, w, tabs, min(TM_IN_PROJ, t))
    oa4 = _swa_prompt(q4, k, v, w["swa_sink"], b, t)
    ob, h_last, conv8 = _lru(xb, gy, jnp.zeros((b, 8, LRU_WIDTH), F32),
                             jnp.zeros((b, 1, LRU_WIDTH), F32), w, 1, min(512, t), b)
    conv_l = conv8[:, 8 - (LRU_CONV_W - 1):]
    oc, st = _gla_prompt(gq, gk, gv, sr, la, w["gla_on_g"], b, t)
    x1, qx = _merge(oa4, ob, oc, gates, x, w, tm)
    mk, mv = _memkv(mem, w, min(512, mem.shape[0]))
    x3, fconv8 = _ffn(qx, x1, jnp.zeros((b, 8, 2 * D_FF), F32), w, 1, tm, b, mem=(mk, mv))
    fconv = fconv8[:, 8 - (FFN_CONV_W - 1):]

    k3 = k.reshape(b, t, SWA_KV_HEADS, SWA_HEAD_DIM)[:, -SWA_WINDOW:]
    v3 = v.reshape(b, t, SWA_KV_HEADS, SWA_HEAD_DIM)[:, -SWA_WINDOW:]
    st5 = st.reshape(b, GLA_HEADS // 2, 2, GLA_DV, 2, GLA_DK)
    s_new = jnp.stack([st5[:, :, 0, :, 0, :], st5[:, :, 1, :, 1, :]], axis=2)
    s_new = s_new.reshape(b, GLA_HEADS, GLA_DV, GLA_DK).transpose(0, 1, 3, 2)
    states = (k3, v3, h_last.reshape(b, LRU_WIDTH), conv_l, s_new,
              mk.reshape(b, n_mem, MEM_HEADS, MEM_HEAD_DIM),
              mv.reshape(b, n_mem, MEM_HEADS, MEM_HEAD_DIM), fconv)
    return x3, states


def _tm_to_bm(a, bs, ts):
    return a.reshape(ts, bs, a.shape[-1]).transpose(1, 0, 2)


def _pad8(a):
    return jnp.concatenate([a, jnp.zeros((a.shape[0], 8 - a.shape[1], a.shape[2]), a.dtype)], axis=1)


def _sample_layer(x, w, tabs, st_in, bs, ts, layer):
    (ck, cv, lru_h, lru_conv, gla_s, mem_k, mem_v, ffn_conv) = st_in
    m = bs * ts
    tm = m
    (q4, k, v, xb, gy, gq, gk, gv, sr, la, gates) = _in_proj(x, w, tabs, tm)

    q8 = q4.reshape(4, ts, bs, 2, SWA_HEAD_DIM).transpose(2, 0, 3, 1, 4)
    q8 = q8.reshape(bs, SWA_HEADS, ts, SWA_HEAD_DIM)
    zq = jnp.zeros_like(q8[:, :4])
    qm = jnp.concatenate([jnp.concatenate([q8[:, :4], zq], axis=-1),
                          jnp.concatenate([zq, q8[:, 4:]], axis=-1)], axis=1)
    qm = qm.reshape(bs, SWA_HEADS * ts, LANES)
    k_new = _tm_to_bm(k, bs, ts)
    v_new = _tm_to_bm(v, bs, ts)
    ckf = ck.reshape(bs, SWA_WINDOW, SWA_KV)
    cvf = cv.reshape(bs, SWA_WINDOW, SWA_KV)
    sink_rows = jnp.repeat(w["swa_sink"], ts)[:, None]
    o32 = _swa_step(qm, ckf, _pad8(k_new), cvf, _pad8(v_new), sink_rows, ts)
    o8 = o32.reshape(bs, SWA_HEADS, ts, 2, SWA_HEAD_DIM)
    osel = jnp.concatenate([o8[:, :4, :, 0], o8[:, 4:, :, 1]], axis=1)
    oa4 = osel.reshape(bs, 4, 2, ts, SWA_HEAD_DIM).transpose(1, 3, 0, 2, 4)
    oa4 = oa4.reshape(4, m, LANES).astype(BF16)
    new_k = jnp.concatenate([ckf, k_new], axis=1)[:, -SWA_WINDOW:]
    new_v = jnp.concatenate([cvf, v_new], axis=1)[:, -SWA_WINDOW:]

    hist = lru_conv.transpose(1, 0, 2).reshape(1, (LRU_CONV_W - 1) * bs, LRU_WIDTH)
    ob, h_last, conv_l = _lru(xb, gy, hist, lru_h[None], w, bs, m, 1)
    new_conv = conv_l.reshape(LRU_CONV_W - 1, bs, LRU_WIDTH).transpose(1, 0, 2)

    bm8 = lambda a: _pad8(_tm_to_bm(a, bs, ts))
    s0p = gla_s.reshape(bs, GLA_HEADS // 2, 2 * GLA_DK, GLA_DV)
    oc8, sn = _gla_step(bm8(gq), bm8(gk), bm8(gv), bm8(sr), bm8(la), s0p, w["gla_on_g"], ts)
    oc = oc8[:, :ts].transpose(1, 0, 2).reshape(m, GLA_V)
    new_s = sn.reshape(bs, GLA_HEADS, GLA_DK, GLA_DV)

    x1, qx = _merge(oa4, ob, oc, gates, x, w, tm)
    qb = qx.reshape(ts, bs, MEM_HEADS, MEM_HEAD_DIM).transpose(1, 2, 0, 3)
    eye = jnp.eye(MEM_HEADS, dtype=qb.dtype)
    qexp = (qb[:, :, :, None, :] * eye[None, :, None, :, None]).reshape(bs, MEM_HEADS * ts, MEM_W)
    o5 = _xattn_step(qexp, mem_k, mem_v, layer).reshape(bs, MEM_HEADS, ts, MEM_HEADS, MEM_HEAD_DIM)
    osel = jnp.stack([o5[:, h, :, h, :] for h in range(MEM_HEADS)], axis=2)
    ox = osel.transpose(1, 0, 2, 3).reshape(m, MEM_W).astype(BF16)
    fh = ffn_conv.transpose(1, 0, 2).reshape(1, (FFN_CONV_W - 1) * bs, 2 * D_FF)
    x3, fconv = _ffn(ox, x1, fh, w, bs, m, 1)
    new_fconv = fconv.reshape(FFN_CONV_W - 1, bs, 2 * D_FF).transpose(1, 0, 2)

    states = (new_k.reshape(bs, SWA_WINDOW, SWA_KV_HEADS, SWA_HEAD_DIM),
              new_v.reshape(bs, SWA_WINDOW, SWA_KV_HEADS, SWA_HEAD_DIM),
              h_last.reshape(bs, LRU_WIDTH), new_conv, new_s, new_fconv)
    return x3, states


def kernel(x_prompt, x_sample, cache_swa_k, cache_swa_v, state_lru_h, state_lru_conv, state_gla_s, cache_mem_k, cache_mem_v, state_ffn_conv, mem_prompt, norm_mix_g, w_in, swa_qn_g, swa_kn_g, swa_sink, lru_conv_w, lru_conv_b, lru_wa, lru_ba, lru_wx, lru_bx, lru_lambda, gla_wa2, gla_ba, gla_on_g, w_branch_a, w_branch_b, w_branch_c, w_out, norm_x_g, norm_mem_g, x_wq, x_wk, x_wv, x_qn_g, x_kn_g, x_wo, norm_ffn_g, ffn_w_up, ffn_conv_w, ffn_conv_b, ffn_w_down):
    p = dict(norm_mix_g=norm_mix_g, w_in=w_in, swa_qn_g=swa_qn_g, swa_kn_g=swa_kn_g, swa_sink=swa_sink,
             lru_conv_w=lru_conv_w, lru_conv_b=lru_conv_b, lru_wa=lru_wa, lru_ba=lru_ba, lru_wx=lru_wx,
             lru_bx=lru_bx, lru_lambda=lru_lambda, gla_wa2=gla_wa2, gla_ba=gla_ba, gla_on_g=gla_on_g,
             w_branch_a=w_branch_a, w_branch_b=w_branch_b, w_branch_c=w_branch_c, w_out=w_out,
             norm_x_g=norm_x_g, norm_mem_g=norm_mem_g, x_wq=x_wq, x_wk=x_wk, x_wv=x_wv, x_qn_g=x_qn_g,
             x_kn_g=x_kn_g, x_wo=x_wo, norm_ffn_g=norm_ffn_g, ffn_w_up=ffn_w_up, ffn_conv_w=ffn_conv_w,
             ffn_conv_b=ffn_conv_b, ffn_w_down=ffn_w_down)
    depth = w_in.shape[0]
    bp, tp = x_prompt.shape[:2]
    bs, ts = x_sample.shape[:2]
    tabs_p = _rope_tables(jnp.arange(tp, dtype=jnp.int32))
    tabs_s = _rope_tables(jnp.repeat(PAST_LEN + jnp.arange(ts, dtype=jnp.int32), bs))

    xp = x_prompt.reshape(bp * tp, D_MODEL)
    xs = x_sample.transpose(1, 0, 2).reshape(ts * bs, D_MODEL)
    mem = mem_prompt.reshape(-1, D_MODEL)
    mem_k_all, mem_v_all = cache_mem_k, cache_mem_v
    sp, ss = [], []
    for l in range(depth):
        w = _layer_weights(p, l)
        xp, st = _prompt_layer(xp, mem, w, tabs_p, bp, tp)
        sp.append(st)
        xs, st = _sample_layer(xs, w, tabs_s,
                               (cache_swa_k[l], cache_swa_v[l], state_lru_h[l], state_lru_conv[l],
                                state_gla_s[l], mem_k_all, mem_v_all, state_ffn_conv[l]),
                               bs, ts, l)
        ss.append(st)
    y_p = xp.reshape(bp, tp, D_MODEL)
    y_s = xs.reshape(ts, bs, D_MODEL).transpose(1, 0, 2)
    stack = lambda lst, i: jnp.stack([s[i] for s in lst])
    return (y_p, y_s,
            stack(sp, 0), stack(sp, 1), stack(sp, 2), stack(sp, 3), stack(sp, 4), stack(sp, 5),
            stack(sp, 6), stack(sp, 7),
            stack(ss, 0), stack(ss, 1), stack(ss, 2), stack(ss, 3), stack(ss, 4), stack(ss, 5))
```
